```python
import jax, jax.numpy as jnp
from jax import lax
import numpy as np

D_MODEL = 1024
BATCH = 16
SEQ = 2048
DEPTH = 2
DEC_BATCH = 128
DEC_SEQ = 8
PAST_LEN = 8192
PAGE_SIZE = 128

MIX_WIDTH = D_MODEL
H_MLA = 8
Q_LORA = 384
KV_LORA = 256
NOPE_DIM = 64
ROPE_DIM = 32
V_DIM = 64
H_NSA = 4
NSA_DIM = 64
N_NSA_KV = 6
N_NSA_BR = 3
L_CMP = 64
N_SEL = 16
WINDOW = 512
H_FOX = 4
FOX_DIM = 64
D_FF = 2816
N_SUB = 3
ROPE_THETA = 10000.0
Q_BLOCK = 128
SEL_Q_CHUNK = 64
FORCED_SCORE = 1.0e4
EPS = 1e-6
MLA_SCALE = (NOPE_DIM + ROPE_DIM) ** -0.5
NSA_SCALE = NSA_DIM ** -0.5
FOX_SCALE = FOX_DIM ** -0.5
IN_SPLITS = (Q_LORA, KV_LORA, ROPE_DIM,
             H_NSA * NSA_DIM, N_NSA_KV * NSA_DIM, H_NSA * N_NSA_BR,
             H_FOX * FOX_DIM, H_FOX * FOX_DIM, H_FOX * FOX_DIM, H_FOX)
N_IN = sum(IN_SPLITS)

kernel_name = 'hybrid_mla_nsa_fox_decoder_step'


def rmsnorm(x, g):
    xf = x.astype(jnp.float32)
    y = xf * lax.rsqrt(jnp.mean(xf * xf, axis=-1, keepdims=True) + EPS)
    return (y * g.astype(jnp.float32)).astype(x.dtype)


def rope(x, pos):
    half = x.shape[-1] // 2
    inv = jnp.power(ROPE_THETA, -jnp.arange(half, dtype=jnp.float32) / half)
    ang = pos.astype(jnp.float32)[:, None] * inv[None, :]
    ang = ang.reshape((pos.shape[0],) + (1,) * (x.ndim - 3) + (half,))
    cos, sin = jnp.cos(ang), jnp.sin(ang)
    xf = x.astype(jnp.float32)
    x1, x2 = xf[..., :half], xf[..., half:]
    return jnp.concatenate([x1 * cos - x2 * sin, x2 * cos + x1 * sin], axis=-1).astype(x.dtype)


def masked_softmax(s, mask):
    s = jnp.where(mask, s.astype(jnp.float32), -jnp.inf)
    m = jnp.max(s, axis=-1, keepdims=True)
    m = jnp.where(jnp.isfinite(m), m, 0.0)
    e = jnp.where(mask, jnp.exp(s - m), 0.0)
    d = jnp.sum(e, axis=-1, keepdims=True)
    return e / jnp.where(d > 0, d, 1.0)


def ada_mods(c, w, b):
    m = jax.nn.silu(c) @ w + b
    return m.reshape(c.shape[0], N_SUB, 3, D_MODEL)


def modulate(x, m):
    return x * (1.0 + m[:, 1][:, None, :]) + m[:, 0][:, None, :]


def ffn_half(h, mods, j, g_pre, g_post, wg, wu, wd):
    u = modulate(rmsnorm(h, g_pre), mods[:, j])
    y = (jax.nn.silu(u @ wg) * (u @ wu)) @ wd
    return h + 0.5 * mods[:, j, 2][:, None, :] * rmsnorm(y, g_post)


def in_proj(u, pos, lw):
    B, T, _ = u.shape
    z = u @ lw['w_in']
    cuts, acc = [], 0
    for n in IN_SPLITS[:-1]:
        acc += n
        cuts.append(acc)
    cq, ckv, kr, nq, nkv, ng, fq, fk, fv, ff = jnp.split(z, cuts, axis=-1)
    q = (rmsnorm(cq, lw['g_q']) @ lw['w_uq']).reshape(B, T, H_MLA, NOPE_DIM + ROPE_DIM)
    nkv = nkv.reshape(B, T, N_NSA_BR, 2, NSA_DIM)
    nkv = jnp.stack([rope(nkv[:, :, :, 0], pos), nkv[:, :, :, 1]], axis=3).reshape(B, T, N_NSA_KV, NSA_DIM)
    return dict(
        q_nope=q[..., :NOPE_DIM],
        q_rope=rope(q[..., NOPE_DIM:], pos),
        ckv=rmsnorm(ckv, lw['g_kv']),
        k_rope=rope(kr, pos),
        nq=rope(nq.reshape(B, T, H_NSA, NSA_DIM), pos),
        nkv=nkv,
        ng=jax.nn.sigmoid(ng.reshape(B, T, H_NSA, N_NSA_BR).astype(jnp.float32)).astype(u.dtype),
        fq=fq.reshape(B, T, H_FOX, FOX_DIM),
        fk=fk.reshape(B, T, H_FOX, FOX_DIM),
        fv=fv.reshape(B, T, H_FOX, FOX_DIM),
        logf=jax.nn.log_sigmoid(ff.astype(jnp.float32) + lw['b_f'].astype(jnp.float32)),
    )


def causal_attn_blocked(q, k, v, scale, logF=None):
    B, T, H, _ = q.shape
    nqb = T // Q_BLOCK
    qb = q.reshape(B, nqb, Q_BLOCK, H, q.shape[-1]).swapaxes(0, 1)
    kpos = jnp.arange(T)
    FT = None if logF is None else logF.transpose(0, 2, 1)

    def block(args):
        qi, i = args
        s = jnp.einsum('bqhd,bkhd->bhqk', qi, k).astype(jnp.float32) * scale
        qpos = i * Q_BLOCK + jnp.arange(Q_BLOCK)
        if FT is not None:
            Fq = lax.dynamic_slice_in_dim(FT, i * Q_BLOCK, Q_BLOCK, axis=2)
            s = s + (Fq[..., None] - FT[:, :, None, :])
        p = masked_softmax(s, kpos[None, :] <= qpos[:, None])
        return jnp.einsum('bhqk,bkhd->bqhd', p.astype(v.dtype), v)

    o = lax.map(block, (qb, jnp.arange(nqb)))
    return o.swapaxes(0, 1).reshape(B, T, H, v.shape[-1])


def decode_probs(s_past, s_new):
    P, T = s_past.shape[-1], s_new.shape[-1]
    qpos = P + jnp.arange(T)
    kpos = jnp.arange(P + T)
    p = masked_softmax(jnp.concatenate([s_past, s_new], axis=-1), kpos[None, :] <= qpos[:, None])
    return p[..., :P], p[..., P:]


def band_attn(q, k, v, q_pos, k_pos):
    s = jnp.einsum('bnqhd,bnkd->bnqhk', q, k).astype(jnp.float32) * NSA_SCALE
    diff = q_pos[:, :, None] - k_pos[:, None, :]
    mask = (diff >= 0) & (diff <= WINDOW) & (k_pos[:, None, :] >= 0)
    p = masked_softmax(s, mask[None, :, :, None, :])
    return jnp.einsum('bnqhk,bnkd->bnqhd', p.astype(v.dtype), v)


def nsa_cmp_sel(q, kc, vc, ks, vs, q_pos, w_cmp):
    B, T, H, D = q.shape
    L = kc.shape[1]
    nb = -(-L // L_CMP)

    def blocks(a):
        return jnp.pad(a, ((0, 0), (0, nb * L_CMP - L), (0, 0))).reshape(B, nb, L_CMP, D)

    k_cmp = jnp.einsum('bnld,lde->bne', blocks(kc), w_cmp[0])
    v_cmp = jnp.einsum('bnld,lde->bne', blocks(vc), w_cmp[1])
    s = jnp.einsum('bthd,bnd->bthn', q, k_cmp).astype(jnp.float32) * NSA_SCALE
    bi = jnp.arange(nb)
    done = (bi[None, :] + 1) * L_CMP - 1 <= q_pos[:, None]
    p_cmp = masked_softmax(s, done[None, :, None, :])
    o_cmp = jnp.einsum('bthn,bnd->bthd', p_cmp.astype(v_cmp.dtype), v_cmp)
    cur = (q_pos // L_CMP)[:, None]
    forced = (bi[None, :] == 0) | (bi[None, :] == cur) | (bi[None, :] == cur - 1)
    imp = jnp.where(forced, FORCED_SCORE, jnp.sum(p_cmp, axis=2))
    imp = jnp.where(bi[None, :] <= cur, imp, -jnp.inf)
    n_sel = min(N_SEL, nb)
    top, idx = lax.top_k(imp, n_sel)
    ok = jnp.isfinite(top)
    ksb, vsb = blocks(ks), blocks(vs)
    qc = SEL_Q_CHUNK if T % SEL_Q_CHUNK == 0 else T
    nc = T // qc
    bidx = jnp.arange(B)[:, None, None]
    lpos = jnp.arange(L_CMP)

    def chunk(args):
        qi, ii, oki, pi = args
        kg, vg = ksb[bidx, ii], vsb[bidx, ii]
        sc = jnp.einsum('bqhd,bqkld->bqhkl', qi, kg).astype(jnp.float32) * NSA_SCALE
        tok = ii[..., None] * L_CMP + lpos
        m = oki[..., None] & (tok <= pi[None, :, None, None])
        pr = masked_softmax(sc.reshape(B, qc, H, n_sel * L_CMP), m.reshape(B, qc, 1, n_sel * L_CMP))
        return jnp.einsum('bqhkl,bqkld->bqhd', pr.reshape(B, qc, H, n_sel, L_CMP).astype(vg.dtype), vg)

    def split(a):
        return a.reshape((B, nc, qc) + a.shape[2:]).swapaxes(0, 1)

    o_sel = lax.map(chunk, (split(q), split(idx), split(ok), q_pos.reshape(nc, qc)))
    return o_cmp, o_sel.swapaxes(0, 1).reshape(B, T, H, D)


def nsa_combine(g, o_cmp, o_sel, o_win):
    return g[..., 0:1] * o_cmp + g[..., 1:2] * o_sel + g[..., 2:3] * o_win


def out_proj(o_mla, o_nsa, o_fox, w_out):
    B, T = o_mla.shape[:2]
    o = jnp.concatenate([o_mla.reshape(B, T, -1), o_nsa.reshape(B, T, -1), o_fox.reshape(B, T, -1)], axis=-1)
    return o @ w_out


def mixer_prompt(u, lw):
    B, T, _ = u.shape
    pos = jnp.arange(T, dtype=jnp.int32)
    z = in_proj(u, pos, lw)
    k_nope = jnp.einsum('btc,chd->bthd', z['ckv'], lw['w_uk'])
    v = jnp.einsum('btc,chd->bthd', z['ckv'], lw['w_uv'])
    qf = jnp.concatenate([z['q_nope'], z['q_rope']], axis=-1)
    kf = jnp.concatenate([k_nope, jnp.broadcast_to(z['k_rope'][:, :, None, :], (B, T, H_MLA, ROPE_DIM))], axis=-1)
    o_mla = causal_attn_blocked(qf, kf, v, MLA_SCALE)
    nkv = z['nkv']
    o_cmp, o_sel = nsa_cmp_sel(z['nq'], nkv[:, :, 0], nkv[:, :, 1], nkv[:, :, 2], nkv[:, :, 3], pos, lw['w_cmp'])
    nqb = T // Q_BLOCK
    band = jnp.arange(nqb)[:, None] * Q_BLOCK + jnp.arange(WINDOW + Q_BLOCK)[None, :]
    wpad = jnp.pad(nkv[:, :, 4:], ((0, 0), (WINDOW, 0), (0, 0), (0, 0)))[:, band]
    o_win = band_attn(z['nq'].reshape(B, nqb, Q_BLOCK, H_NSA, NSA_DIM), wpad[..., 0, :], wpad[..., 1, :],
                      pos.reshape(nqb, Q_BLOCK), band - WINDOW).reshape(B, T, H_NSA, NSA_DIM)
    o_nsa = nsa_combine(z['ng'], o_cmp, o_sel, o_win)
    F = jnp.cumsum(z['logf'], axis=1)
    o_fox = causal_attn_blocked(z['fq'], z['fk'], z['fv'], FOX_SCALE, F)
    y = out_proj(o_mla, o_nsa, o_fox, lw['w_out'])
    wb = min(WINDOW, T)
    state = (z['ckv'], z['k_rope'], nkv[:, :, :4], nkv[:, T - wb:, 4:],
             jnp.stack([z['fk'], z['fv']], axis=2), z['logf'])
    return y, state


def mixer_sample(u, lw, ckv_past, kr_past, nsa_past, win_past, fkv_past, flf_past):
    B, T, _ = u.shape
    P = ckv_past.shape[1]
    pos = P + jnp.arange(T, dtype=jnp.int32)
    z = in_proj(u, pos, lw)
    q_lat = jnp.einsum('bthd,chd->bthc', z['q_nope'], lw['w_uk'])

    def mla_scores(ckv, kr):
        s = jnp.einsum('bthc,bsc->bhts', q_lat, ckv) + jnp.einsum('bthr,bsr->bhts', z['q_rope'], kr)
        return s.astype(jnp.float32) * MLA_SCALE

    pp, pn = decode_probs(mla_scores(ckv_past, kr_past), mla_scores(z['ckv'], z['k_rope']))
    o_lat = (jnp.einsum('bhts,bsc->bthc', pp.astype(ckv_past.dtype), ckv_past)
             + jnp.einsum('bhts,bsc->bthc', pn.astype(z['ckv'].dtype), z['ckv']))
    o_mla = jnp.einsum('bthc,chd->bthd', o_lat, lw['w_uv'])
    nkv = z['nkv']
    allkv = jnp.concatenate([nsa_past, nkv[:, :, :4]], axis=1)
    o_cmp, o_sel = nsa_cmp_sel(z['nq'], allkv[:, :, 0], allkv[:, :, 1], allkv[:, :, 2], allkv[:, :, 3], pos, lw['w_cmp'])
    WB = win_past.shape[1]
    win_all = jnp.concatenate([win_past, nkv[:, :, 4:]], axis=1)
    k_pos = P - WB + jnp.arange(WB + T)
    o_win = band_attn(z['nq'][:, None], win_all[:, None, :, 0], win_all[:, None, :, 1], pos[None], k_pos[None])[:, 0]
    o_nsa = nsa_combine(z['ng'], o_cmp, o_sel, o_win)
    lf_past = flf_past.astype(jnp.float32)
    R = jnp.flip(jnp.cumsum(jnp.flip(lf_past, 1), axis=1), 1) - lf_past
    Gt = jnp.cumsum(z['logf'], axis=1).transpose(0, 2, 1)
    k_past, v_past = fkv_past[:, :, 0], fkv_past[:, :, 1]
    s_past = (jnp.einsum('bthd,bshd->bhts', z['fq'], k_past).astype(jnp.float32) * FOX_SCALE
              + Gt[..., None] + R.transpose(0, 2, 1)[:, :, None, :])
    s_new = (jnp.einsum('bthd,bshd->bhts', z['fq'], z['fk']).astype(jnp.float32) * FOX_SCALE
             + Gt[..., None] - Gt[:, :, None, :])
    pp, pn = decode_probs(s_past, s_new)
    o_fox = (jnp.einsum('bhts,bshd->bthd', pp.astype(v_past.dtype), v_past)
             + jnp.einsum('bhts,bshd->bthd', pn.astype(z['fv'].dtype), z['fv']))
    y = out_proj(o_mla, o_nsa, o_fox, lw['w_out'])
    state = (z['ckv'], z['k_rope'], nkv[:, :, :4], win_all[:, T:],
             jnp.stack([z['fk'], z['fv']], axis=2), z['logf'])
    return y, state


def setup_inputs(seed: int = 0) -> dict:
    key = jax.random.key(seed)
    k = jax.random.split(key, 32)
    f32 = jnp.float32

    def nrm(kk, shape, s=1.0):
        return jax.random.normal(kk, shape, f32) * s

    n_pages = PAST_LEN // PAGE_SIZE
    n_pool = (5 * DEC_BATCH * n_pages) // 4
    win_buf = min(WINDOW, PAST_LEN)
    page_table = jax.random.permutation(k[0], n_pool)[: DEC_BATCH * n_pages].reshape(DEC_BATCH, n_pages).astype(jnp.int32)
    return {
        'x_prompt': nrm(k[1], (BATCH, SEQ, D_MODEL)),
        'x_sample': nrm(k[2], (DEC_BATCH, DEC_SEQ, D_MODEL)),
        'cache_mla_ckv': nrm(k[3], (DEPTH, n_pool, PAGE_SIZE, KV_LORA)),
        'cache_mla_krope': nrm(k[4], (DEPTH, n_pool, PAGE_SIZE, ROPE_DIM)),
        'cache_nsa_kv': nrm(k[5], (DEPTH, n_pool, PAGE_SIZE, 4, NSA_DIM)),
        'state_nsa_win': nrm(k[6], (DEPTH, DEC_BATCH, win_buf, 2, NSA_DIM)),
        'cache_fox_kv': nrm(k[7], (DEPTH, n_pool, PAGE_SIZE, 2, H_FOX, FOX_DIM)),
        'cache_fox_logf': jax.nn.log_sigmoid(3.0 + nrm(k[8], (DEPTH, n_pool, PAGE_SIZE, H_FOX))),
        'page_table': page_table,
        'c_prompt': nrm(k[9], (BATCH, D_MODEL)),
        'c_sample': nrm(k[10], (DEC_BATCH, D_MODEL)),
        'ada_w': nrm(k[11], (DEPTH, D_MODEL, N_SUB * 3 * D_MODEL), 0.5 * D_MODEL ** -0.5),
        'ada_b': nrm(k[12], (DEPTH, N_SUB * 3 * D_MODEL), 0.02),
        'norm_pre': 1.0 + nrm(k[13], (DEPTH, N_SUB, D_MODEL), 0.05),
        'norm_post': 1.0 + nrm(k[14], (DEPTH, N_SUB, D_MODEL), 0.05),
        'ffn_w_gate': nrm(k[15], (DEPTH, 2, D_MODEL, D_FF), D_MODEL ** -0.5),
        'ffn_w_up': nrm(k[16], (DEPTH, 2, D_MODEL, D_FF), D_MODEL ** -0.5),
        'ffn_w_down': nrm(k[17], (DEPTH, 2, D_FF, D_MODEL), D_FF ** -0.5),
        'w_in': nrm(k[18], (DEPTH, D_MODEL, N_IN), D_MODEL ** -0.5),
        'b_fox_f': 3.0 + nrm(k[19], (DEPTH, H_FOX), 0.5),
        'mla_g_q': 1.0 + nrm(k[20], (DEPTH, Q_LORA), 0.05),
        'mla_g_kv': 1.0 + nrm(k[21], (DEPTH, KV_LORA), 0.05),
        'mla_w_uq': nrm(k[22], (DEPTH, Q_LORA, H_MLA * (NOPE_DIM + ROPE_DIM)), Q_LORA ** -0.5),
        'mla_w_uk': nrm(k[23], (DEPTH, KV_LORA, H_MLA, NOPE_DIM), KV_LORA ** -0.5),
        'mla_w_uv': nrm(k[24], (DEPTH, KV_LORA, H_MLA, V_DIM), KV_LORA ** -0.5),
        'nsa_w_cmp': nrm(k[25], (DEPTH, 2, L_CMP, NSA_DIM, NSA_DIM), (L_CMP * NSA_DIM) ** -0.5),
        'w_out': nrm(k[26], (DEPTH, MIX_WIDTH, D_MODEL), MIX_WIDTH ** -0.5),
    }


def reference(x_prompt, x_sample, cache_mla_ckv, cache_mla_krope, cache_nsa_kv, state_nsa_win,
              cache_fox_kv, cache_fox_logf, page_table, c_prompt, c_sample,
              ada_w, ada_b, norm_pre, norm_post, ffn_w_gate, ffn_w_up, ffn_w_down,
              w_in, b_fox_f, mla_g_q, mla_g_kv, mla_w_uq, mla_w_uk, mla_w_uv, nsa_w_cmp, w_out):
    DB = page_table.shape[0]
    P = page_table.shape[1] * cache_mla_ckv.shape[2]
    hp, hs = x_prompt, x_sample
    st_p, st_s = [], []
    for l in range(DEPTH):
        lw = dict(w_in=w_in[l], b_f=b_fox_f[l], g_q=mla_g_q[l], g_kv=mla_g_kv[l], w_uq=mla_w_uq[l],
                  w_uk=mla_w_uk[l], w_uv=mla_w_uv[l], w_cmp=nsa_w_cmp[l], w_out=w_out[l])
        mp = ada_mods(c_prompt, ada_w[l], ada_b[l])
        ms = ada_mods(c_sample, ada_w[l], ada_b[l])
        hp = ffn_half(hp, mp, 0, norm_pre[l, 0], norm_post[l, 0], ffn_w_gate[l, 0], ffn_w_up[l, 0], ffn_w_down[l, 0])
        hs = ffn_half(hs, ms, 0, norm_pre[l, 0], norm_post[l, 0], ffn_w_gate[l, 0], ffn_w_up[l, 0], ffn_w_down[l, 0])
        yp, sp = mixer_prompt(modulate(rmsnorm(hp, norm_pre[l, 1]), mp[:, 1]), lw)
        hp = hp + mp[:, 1, 2][:, None, :] * rmsnorm(yp, norm_post[l, 1])
        ckv_past = cache_mla_ckv[l, page_table].reshape(DB, P, KV_LORA)
        kr_past = cache_mla_krope[l, page_table].reshape(DB, P, ROPE_DIM)
        nsa_past = cache_nsa_kv[l, page_table].reshape(DB, P, 4, NSA_DIM)
        fkv_past = cache_fox_kv[l, page_table].reshape(DB, P, 2, H_FOX, FOX_DIM)
        flf_past = cache_fox_logf[l, page_table].reshape(DB, P, H_FOX)
        ys, ss = mixer_sample(modulate(rmsnorm(hs, norm_pre[l, 1]), ms[:, 1]), lw, ckv_past, kr_past,
                              nsa_past, state_nsa_win[l], fkv_past, flf_past)
        hs = hs + ms[:, 1, 2][:, None, :] * rmsnorm(ys, norm_post[l, 1])
        hp = ffn_half(hp, mp, 2, norm_pre[l, 2], norm_post[l, 2], ffn_w_gate[l, 1], ffn_w_up[l, 1], ffn_w_down[l, 1])
        hs = ffn_half(hs, ms, 2, norm_pre[l, 2], norm_post[l, 2], ffn_w_gate[l, 1], ffn_w_up[l, 1], ffn_w_down[l, 1])
        st_p.append(sp)
        st_s.append(ss)
    p_ckv, p_kr, p_nsa, p_win, p_fkv, p_flf = (jnp.stack(a) for a in zip(*st_p))
    s_ckv, s_kr, s_nsa, s_win, s_fkv, s_flf = (jnp.stack(a) for a in zip(*st_s))
    return (hp, hs, p_ckv, p_kr, p_nsa, p_win, p_fkv, p_flf, s_ckv, s_kr, s_nsa, s_win, s_fkv, s_flf)
```

```python
import functools

import numpy as np
import jax
import jax.numpy as jnp
from jax import lax
from jax.experimental import pallas as pl
from jax.experimental.pallas import tpu as pltpu

F32, BF16 = jnp.float32, jnp.bfloat16

H_MLA, Q_LORA, KV_LORA, NOPE_DIM, ROPE_DIM, V_DIM = 8, 384, 256, 64, 32, 64
H_NSA, NSA_DIM, L_CMP, N_SEL, WINDOW = 4, 64, 64, 16, 512
H_FOX, FOX_DIM = 4, 64
N_SUB = 3
ROPE_THETA = 10000.0
FORCED_SCORE = 1.0e4
EPS = 1e-6
MLA_SCALE = (NOPE_DIM + ROPE_DIM) ** -0.5
NSA_SCALE = NSA_DIM ** -0.5
FOX_SCALE = FOX_DIM ** -0.5
IN_SPLITS = (Q_LORA, KV_LORA, ROPE_DIM, H_NSA * NSA_DIM, 6 * NSA_DIM, H_NSA * 3,
             H_FOX * FOX_DIM, H_FOX * FOX_DIM, H_FOX * FOX_DIM, H_FOX)

LANES = 128
VMEM_LIMIT = 56 * 1024 * 1024

MISC_GATE = ROPE_DIM
MISC_LOGF = ROPE_DIM + H_NSA * 3
MISC_END = MISC_LOGF + H_FOX
FOX_XW = 6

_W_WIDTHS = (("cq", Q_LORA), ("ckv", KV_LORA), ("nq", 256), ("nqs", 256), ("nkv", 384), ("nkvs", 384),
             ("fq", 256), ("fkv", 512), ("misc", 128), ("miscs", 128), ("krp", 128), ("krps", 128))
_W_OFF = {}
_acc = 0
for _n, _w in _W_WIDTHS:
    _W_OFF[_n] = (_acc, _acc + _w)
    _acc += _w
W_TOTAL = _acc
_TAB = {n: (i * LANES, (i + 1) * LANES) for i, n in
        enumerate(("ckx", "skx", "cm", "sm", "cn", "sn", "cnkv", "snkv"))}
TAB_TOTAL = 8 * LANES


def _cparams(sem):
    return pltpu.CompilerParams(dimension_semantics=sem, vmem_limit_bytes=VMEM_LIMIT)


def _rms(x, g):
    ms = jnp.mean(x * x, axis=-1, keepdims=True)
    return x * lax.rsqrt(ms + EPS) * g


def _silu(x):
    return x / (1.0 + jnp.exp(-x))


def _softmax_rows(s):
    m = jnp.max(s, axis=-1, keepdims=True)
    m = jnp.where(m > -jnp.inf, m, 0.0)
    e = jnp.exp(s - m)
    d = jnp.sum(e, axis=-1, keepdims=True)
    return e * (1.0 / jnp.where(d > 0, d, 1.0))


def _dot(a, b):
    return jnp.dot(a, b, preferred_element_type=F32)


def _dot_nt(a, b):
    return lax.dot_general(a, b, (((1,), (1,)), ((), ())), preferred_element_type=F32)


def _split3(x):
    hi = x.astype(BF16)
    r1 = x - hi.astype(F32)
    mid = r1.astype(BF16)
    lo = (r1 - mid.astype(F32)).astype(BF16)
    return hi, mid, lo


def _ada_body(c_ref, w_ref, b_ref, o_ref):
    s = _silu(c_ref[...])
    o_ref[0, 0] = _dot(s.astype(BF16), w_ref[0].astype(BF16)) + b_ref[0, 0]


def ada_mods(c_all, ada_w, ada_b):
    depth, d, n9 = ada_w.shape
    nc = n9 // d
    bt = c_all.shape[0]
    return pl.pallas_call(
        _ada_body,
        grid=(depth, nc),
        in_specs=[pl.BlockSpec((bt, d), lambda l, n: (0, 0)),
                  pl.BlockSpec((1, d, d), lambda l, n: (l, 0, n)),
                  pl.BlockSpec((1, 1, 1, d), lambda l, n: (l, n, 0, 0))],
        out_specs=pl.BlockSpec((1, 1, bt, d), lambda l, n: (l, n, 0, 0)),
        out_shape=jax.ShapeDtypeStruct((depth, nc, bt, d), F32),
        compiler_params=_cparams(("arbitrary", "arbitrary")),
        name="ada_mods",
    )(c_all, ada_w, ada_b.reshape(depth, nc, 1, d))


def _ffn_body(h_ref, m_ref, gpre_ref, gpost_ref, wg_ref, wu_ref, wd_ref, o_ref, u_scr, acc_scr):
    f = pl.program_id(1)
    g_, r_, d_ = h_ref.shape

    @pl.when(f == 0)
    def _():
        y = _rms(h_ref[...], gpre_ref[...])
        u = y * (1.0 + m_ref[1]) + m_ref[0]
        u_scr[...] = u.reshape(g_ * r_, d_).astype(BF16)
        acc_scr[...] = jnp.zeros_like(acc_scr)

    u = u_scr[...]
    gate = _dot(u, wg_ref[0, 0])
    up = _dot(u, wu_ref[0, 0])
    a = (_silu(gate) * up).astype(BF16)
    acc_scr[...] += _dot(a, wd_ref[0, 0])

    @pl.when(f == pl.num_programs(1) - 1)
    def _():
        yn = _rms(acc_scr[...], gpost_ref[...]).reshape(g_, r_, d_)
        o_ref[...] = h_ref[...] + 0.5 * m_ref[2] * yn


def _tok_tile(n_rows_per_group, cap):
    return min(n_rows_per_group, cap)


def ffn_half(h, mods, sub, g_pre, g_post, wg, wu, wd, layer, half, *, tm_cap=1024, tf=256):
    b, t, d = h.shape
    dff = wg.shape[-1]
    tf = min(tf, dff)
    if t >= 8 * 16:
        tm = _tok_tile(t, tm_cap)
        blk, tpb = (1, tm, d), t // tm
        hmap = lambda i, f: (i // tpb, i % tpb, 0)
        mmap = lambda i, f: (sub, i // tpb, 0, 0)
        grid0, mblk = b * tpb, (3, 1, 1, d)
    else:
        gb = min(b, max(1, tm_cap // t))
        blk = (gb, t, d)
        hmap = lambda i, f: (i, 0, 0)
        mmap = lambda i, f: (sub, i, 0, 0)
        grid0, mblk = b // gb, (3, gb, 1, d)
    rows = blk[0] * blk[1]
    return pl.pallas_call(
        _ffn_body,
        grid=(grid0, dff // tf),
        in_specs=[pl.BlockSpec(blk, hmap),
                  pl.BlockSpec(mblk, mmap),
                  pl.BlockSpec((1, d), lambda i, f: (0, 0)),
                  pl.BlockSpec((1, d), lambda i, f: (0, 0)),
                  pl.BlockSpec((1, 1, d, tf), lambda i, f: (layer, half, 0, f)),
                  pl.BlockSpec((1, 1, d, tf), lambda i, f: (layer, half, 0, f)),
                  pl.BlockSpec((1, 1, tf, d), lambda i, f: (layer, half, f, 0))],
        out_specs=pl.BlockSpec(blk, hmap),
        out_shape=jax.ShapeDtypeStruct(h.shape, F32),
        scratch_shapes=[pltpu.VMEM((rows, d), BF16), pltpu.VMEM((rows, d), F32)],
        compiler_params=_cparams(("arbitrary", "arbitrary")),
        name="ffn_half",
    )(h, mods, g_pre.reshape(1, d), g_post.reshape(1, d), wg, wu, wd)


def _swap_halves(w, dh):
    k = w.shape[0]
    w4 = w.reshape(k, w.shape[1] // dh, 2, dh // 2)
    return w4[:, :, ::-1, :].reshape(k, -1)


def _inproj_weights(w_in, w_uq, w_uk, w_uv):
    k = w_in.shape[0]
    cuts = np.cumsum(IN_SPLITS)[:-1].tolist()
    cq, ckv, kr, nq, nkv, ng, fq, fk, fv, ff = jnp.split(w_in, cuts, axis=1)
    z = lambda n: jnp.zeros((k, n), w_in.dtype)
    nkv3 = nkv.reshape(k, 3, 2, NSA_DIM)
    nkvs = jnp.stack([_swap_halves(nkv3[:, :, 0].reshape(k, -1), NSA_DIM).reshape(k, 3, NSA_DIM),
                      jnp.zeros((k, 3, NSA_DIM), w_in.dtype)], axis=2).reshape(k, -1)
    krs = _swap_halves(kr, ROPE_DIM)
    cols = dict(cq=cq, ckv=ckv, nq=nq, nqs=_swap_halves(nq, NSA_DIM), nkv=nkv, nkvs=nkvs, fq=fq,
                fkv=jnp.concatenate([fk, fv], axis=1),
                misc=jnp.concatenate([kr, ng, ff, z(LANES - MISC_END)], axis=1),
                miscs=jnp.concatenate([krs, z(LANES - ROPE_DIM)], axis=1),
                krp=jnp.concatenate([kr, kr, z(LANES - 2 * ROPE_DIM)], axis=1),
                krps=jnp.concatenate([krs, krs, z(LANES - 2 * ROPE_DIM)], axis=1))
    w_all = jnp.concatenate([cols[n] for n, _ in _W_WIDTHS], axis=1).astype(BF16)
    dq = NOPE_DIM + ROPE_DIM
    uq = w_uq.reshape(Q_LORA, H_MLA, dq)
    wqn = uq[:, :, :NOPE_DIM].reshape(Q_LORA, -1)
    rope = uq[:, :, NOPE_DIM:]
    zr = jnp.zeros((Q_LORA, H_MLA // 2, LANES - 2 * ROPE_DIM), w_uq.dtype)

    def pairs(r):
        return jnp.concatenate([r.reshape(Q_LORA, H_MLA // 2, 2 * ROPE_DIM), zr], axis=2).reshape(Q_LORA, -1)

    wqr = pairs(rope)
    wqs = pairs(_swap_halves(rope.reshape(Q_LORA, -1), ROPE_DIM).reshape(Q_LORA, H_MLA, ROPE_DIM))
    wuk = w_uk.reshape(KV_LORA, -1)
    wuv = w_uv.reshape(KV_LORA, -1)
    ukt = jnp.transpose(w_uk, (1, 2, 0))
    zt = jnp.zeros_like(ukt)
    even = jnp.concatenate([ukt, zt], axis=1)
    odd = jnp.concatenate([zt, ukt], axis=1)
    wukt = jnp.where((jnp.arange(H_MLA) % 2 == 0)[:, None, None], even, odd)
    return dict(w_all=w_all, wqn=wqn.astype(BF16), wqr=wqr.astype(BF16), wqs=wqs.astype(BF16),
                wuk=wuk.astype(BF16), wuv=wuv.astype(BF16), wukt=wukt.astype(BF16))


def _rope_tables(pos):
    def cs(dh):
        half = dh // 2
        inv = jnp.power(ROPE_THETA, -jnp.arange(half, dtype=F32) / half)
        ang = pos.astype(F32)[:, None] * inv[None, :]
        c, s = jnp.cos(ang), jnp.sin(ang)
        return jnp.concatenate([c, c], 1), jnp.concatenate([-s, s], 1)

    n = pos.shape[0]
    c32, s32 = cs(ROPE_DIM)
    c64, s64 = cs(NSA_DIM)
    one, zero = jnp.ones((n, 1), F32), jnp.zeros((n, 1), F32)
    rep = lambda a, k: jnp.tile(a, (1, k))
    tabs = dict(
        ckx=jnp.concatenate([c32, c32, rep(zero, 64)], 1), skx=jnp.concatenate([s32, s32, rep(zero, 64)], 1),
        cm=jnp.concatenate([c32, rep(one, 96)], 1), sm=jnp.concatenate([s32, rep(zero, 96)], 1),
        cn=NSA_SCALE * jnp.concatenate([c64, c64], 1), sn=NSA_SCALE * jnp.concatenate([s64, s64], 1),
        cnkv=jnp.concatenate([c64, rep(one, 64)], 1), snkv=jnp.concatenate([s64, rep(zero, 64)], 1))
    return jnp.concatenate([tabs[k] for k in _TAB], axis=1)


def _fox_place():
    pq = np.zeros((3 * LANES, 2 * LANES), np.float32)
    pk = np.zeros((3 * LANES, 2 * LANES), np.float32)
    cq = np.zeros((1, 2 * LANES), np.float32)
    ck = np.zeros((1, 2 * LANES), np.float32)
    for h in range(H_FOX):
        base = (h // 2) * LANES + (h % 2) * FOX_XW
        for s in range(3):
            pq[s * LANES + MISC_LOGF + h, base + s] = 1.0
            pk[s * LANES + MISC_LOGF + h, base + 3 + s] = -1.0
            cq[0, base + 3 + s] = 1.0
            ck[0, base + s] = 1.0
    return (jnp.asarray(pq, BF16), jnp.asarray(pk, BF16), jnp.asarray(cq), jnp.asarray(ck))


def _inproj_body(decode, tpb, *refs):
    (h_ref, m_ref, gpre_ref, w_ref, gq_ref, gkv_ref, wqn_ref, wqr_ref, wqs_ref, wuk_ref, wuv_ref, wukt_ref,
     tab_ref, bvec_ref, ltri_ref, pq_ref, pk_ref, cq_ref, ck_ref) = refs[:19]
    outs = refs[19:-1]
    carry_scr = refs[-1]
    (ckv_o, nsa_o, win_o, fox_o, misc_o, qn_o, qx_o, kx_o, nq_o, fq_o, fkv16_o) = outs[:11]
    g_, r_, d_ = h_ref.shape
    tm = g_ * r_
    y = _rms(h_ref[...], gpre_ref[...])
    u = (y * (1.0 + m_ref[1]) + m_ref[0]).reshape(tm, d_).astype(BF16)

    def z(name):
        a, b = _W_OFF[name]
        return _dot(u, w_ref[:, a:b])

    def tab(name, k=1):
        a, b = _TAB[name]
        t = tab_ref[:, a:b]
        return t if k == 1 else jnp.concatenate([t] * k, axis=1)

    cqn = _rms(z("cq"), gq_ref[...]).astype(BF16)
    qn = _dot(cqn, wqn_ref[...]) * MLA_SCALE
    qn16 = qn.astype(BF16)
    qn_o[...] = qn16.astype(qn_o.dtype)
    qx = (_dot(cqn, wqr_ref[...]) * tab("ckx", 4) + _dot(cqn, wqs_ref[...]) * tab("skx", 4)) * MLA_SCALE
    qx_o[...] = qx.astype(qx_o.dtype)
    ckvn = _rms(z("ckv"), gkv_ref[...])
    ckv_o[...] = ckvn
    kx_o[...] = (z("krp") * tab("ckx") + z("krps") * tab("skx")).astype(kx_o.dtype)
    if decode:
        qlat_o, fcum_o = outs[11:]
        for h in range(H_MLA):
            p = h // 2
            qlat_o[:, h * KV_LORA:(h + 1) * KV_LORA] = _dot(qn16[:, p * LANES:(p + 1) * LANES], wukt_ref[h])
    else:
        kn_o, vm_o, fqx_o, fkx_o = outs[11:]
        ck16 = ckvn.astype(BF16)
        kn_o[...] = _dot(ck16, wuk_ref[...]).astype(BF16)
        vm_o[...] = _dot(ck16, wuv_ref[...]).astype(BF16)
    nq_o[...] = (z("nq") * tab("cn", 2) + z("nqs") * tab("sn", 2)).astype(nq_o.dtype)
    nkv = z("nkv") * tab("cnkv", 3) + z("nkvs") * tab("snkv", 3)
    nsa_o[...] = nkv[:, :4 * NSA_DIM]
    win_o[...] = nkv[:, 4 * NSA_DIM:]
    fq_o[...] = (z("fq") * FOX_SCALE).astype(fq_o.dtype)
    fkv = z("fkv")
    fox_o[...] = fkv
    fkv16_o[...] = fkv.astype(fkv16_o.dtype)
    zm = z("misc")
    lane = lax.broadcasted_iota(jnp.int32, (tm, LANES), 1)
    roped = zm * tab("cm") + z("miscs") * tab("sm")
    xb = zm + bvec_ref[...]
    logsig = jnp.minimum(xb, 0.0) - jnp.log(1.0 + jnp.exp(-jnp.abs(xb)))
    misc = jnp.where(lane < MISC_GATE, roped,
                     jnp.where(lane < MISC_LOGF, 1.0 / (1.0 + jnp.exp(-zm)),
                               jnp.where(lane < MISC_END, logsig, 0.0)))
    misc_o[...] = misc
    lf = jnp.where((lane >= MISC_LOGF) & (lane < MISC_END), misc, 0.0)
    ltri = ltri_ref[...]
    hi, mid, lo = _split3(lf)
    cum = _dot(ltri, hi) + _dot(ltri, mid) + _dot(ltri, lo)
    first = (pl.program_id(0) % tpb) == 0
    carry = jnp.where(first, 0.0, carry_scr[0:1, :])
    fc = cum + carry
    carry_scr[0:1, :] = fc[tm - 1:tm, :]
    if decode:
        fcum_o[...] = fc
    else:
        g3 = jnp.concatenate(list(_split3(fc)), axis=1)
        fqx_o[...] = (_dot(g3, pq_ref[...]) + cq_ref[...]).astype(BF16)
        fkx_o[...] = (_dot(g3, pk_ref[...]) + ck_ref[...]).astype(BF16)


def in_proj(h, mods, g_pre, lw, tables, bvec, ltri, place, *, decode, tm_cap=512):
    b, t, d = h.shape
    n = b * t
    if not decode:
        tm = _tok_tile(t, tm_cap)
        tpb = t // tm
        blk = (1, tm, d)
        hmap = lambda i: (i // tpb, i % tpb, 0)
        mmap = lambda i: (1, i // tpb, 0, 0)
        tmap = lambda i: (i % tpb, 0)
        grid0, mblk = b * tpb, (3, 1, 1, d)
    else:
        gb = min(b, max(1, tm_cap // t))
        tm, tpb = gb * t, 1
        blk = (gb, t, d)
        hmap = lambda i: (i, 0, 0)
        mmap = lambda i: (1, i, 0, 0)
        tmap = lambda i: (0, 0)
        grid0, mblk = b // gb, (3, gb, 1, d)
    full = lambda a: pl.BlockSpec(a.shape, lambda i: (0,) * a.ndim)
    pq, pk, cq, ck = place
    ins = [h, mods, g_pre.reshape(1, d), lw["w_all"], lw["g_q"].reshape(1, -1), lw["g_kv"].reshape(1, -1),
           lw["wqn"], lw["wqr"], lw["wqs"], lw["wuk"], lw["wuv"], lw["wukt"], tables, bvec, ltri, pq, pk, cq, ck]
    in_specs = [pl.BlockSpec(blk, hmap), pl.BlockSpec(mblk, mmap)] + [full(a) for a in ins[2:12]] + \
               [pl.BlockSpec((tm, TAB_TOTAL), tmap)] + [full(a) for a in ins[13:]]
    names = ["ckv", "nsa", "win", "fox", "misc", "qn", "qx", "kx", "nq", "fq", "fkv16"]
    widths = [KV_LORA, 256, 128, 512, 128, 512, 512, 128, 256, 256, 512]
    dts = [F32] * 5 + [F32 if decode else BF16] * 6
    if decode:
        names += ["qlat", "fcum"]
        widths += [H_MLA * KV_LORA, 128]
        dts += [F32, F32]
    else:
        names += ["kn", "vm", "fqx", "fkx"]
        widths += [512, 512, 256, 256]
        dts += [BF16, BF16, BF16, BF16]
    outs = pl.pallas_call(
        functools.partial(_inproj_body, decode, tpb),
        grid=(grid0,),
        in_specs=in_specs,
        out_specs=[pl.BlockSpec((tm, w), lambda i: (i, 0)) for w in widths],
        out_shape=[jax.ShapeDtypeStruct((n, w), dt) for w, dt in zip(widths, dts)],
        scratch_shapes=[pltpu.VMEM((8, LANES), F32)],
        compiler_params=_cparams(("arbitrary",)),
        name="in_proj_dec" if decode else "in_proj",
    )(*ins)
    return dict(zip(names, outs))


def _flash_body(xw, tq, tk, qa_ref, qx_ref, ka_ref, kx_ref, v_ref, o_ref, m_scr, l_scr, acc_scr):
    i = pl.program_id(2)
    lane = lax.broadcasted_iota(jnp.int32, (tq, LANES), 1)
    qa, qx = qa_ref[0], qx_ref[0]
    zero = jnp.zeros_like(qa)
    q0 = jnp.concatenate([jnp.where(lane < 64, qa, zero), jnp.where(lane < xw, qx, zero)], axis=1)
    q1 = jnp.concatenate([jnp.where(lane >= 64, qa, zero),
                          jnp.where((lane >= xw) & (lane < 2 * xw), qx, zero)], axis=1)
    q2 = jnp.concatenate([q0, q1], axis=0)
    m_scr[...] = jnp.full_like(m_scr, -jnp.inf)
    l_scr[...] = jnp.zeros_like(l_scr)
    acc_scr[...] = jnp.zeros_like(acc_scr)

    def chunk(c, masked):
        k0 = pl.multiple_of(c * tk, tk)
        k2 = jnp.concatenate([ka_ref[0, pl.ds(k0, tk), :], kx_ref[0, pl.ds(k0, tk), :]], axis=1)
        s = _dot_nt(q2, k2)
        if masked:
            row = lax.broadcasted_iota(jnp.int32, (2 * tq, tk), 0)
            col = lax.broadcasted_iota(jnp.int32, (2 * tq, tk), 1)
            qpos = i * tq + jnp.where(row >= tq, row - tq, row)
            s = jnp.where(k0 + col <= qpos, s, -jnp.inf)
        m_prev = m_scr[...]
        m_new = jnp.maximum(m_prev, jnp.max(s, axis=1, keepdims=True))
        alpha = jnp.exp(m_prev - m_new)
        p = jnp.exp(s - m_new[:, :1])
        l_scr[...] = alpha * l_scr[...] + jnp.sum(p, axis=1, keepdims=True)
        acc_scr[...] = alpha * acc_scr[...] + _dot(p.astype(BF16), v_ref[0, pl.ds(k0, tk), :])
        m_scr[...] = m_new

    n_full = (i * tq) // tk

    def body(c, carry):
        chunk(c, False)
        return carry

    lax.fori_loop(0, n_full, body, 0)
    chunk(n_full, True)
    o = acc_scr[...] * (1.0 / l_scr[...])
    o_ref[0] = jnp.where(lane < 64, o[:tq], o[tq:]).astype(o_ref.dtype)


def flash_pairs(qa, qx, ka, kx, v, *, xw, kx_shared, v_off=0, tq=256, tk=512):
    b, t, w = qa.shape
    npair = w // LANES
    tq, tk = min(tq, t), min(tk, t)
    kxmap = (lambda bb, p, i: (bb, 0, 0)) if kx_shared else (lambda bb, p, i: (bb, 0, p))
    return pl.pallas_call(
        functools.partial(_flash_body, xw, tq, tk),
        grid=(b, npair, t // tq),
        in_specs=[pl.BlockSpec((1, tq, LANES), lambda bb, p, i: (bb, i, p)),
                  pl.BlockSpec((1, tq, LANES), lambda bb, p, i: (bb, i, p)),
                  pl.BlockSpec((1, t, LANES), lambda bb, p, i: (bb, 0, p)),
                  pl.BlockSpec((1, t, LANES), kxmap),
                  pl.BlockSpec((1, t, LANES), lambda bb, p, i: (bb, 0, v_off + p))],
        out_specs=pl.BlockSpec((1, tq, LANES), lambda bb, p, i: (bb, i, p)),
        out_shape=jax.ShapeDtypeStruct((b, t, w), BF16),
        scratch_shapes=[pltpu.VMEM((2 * tq, LANES), F32)] * 3,
        compiler_params=_cparams(("arbitrary", "arbitrary", "arbitrary")),
        name="flash_pairs",
    )(qa, qx, ka, kx, v)


def _compress_body(x_ref, w_ref, o_ref):
    @pl.when(pl.program_id(0) == 0)
    def _():
        o_ref[...] = jnp.zeros_like(o_ref)

    o_ref[...] += _dot(x_ref[...].astype(BF16), w_ref[...])


def _cmp_weight(w_cmp):
    z = jnp.zeros((L_CMP, NSA_DIM, NSA_DIM), w_cmp.dtype)
    top = jnp.concatenate([w_cmp[0], z], axis=2)
    bot = jnp.concatenate([z, w_cmp[1]], axis=2)
    zz = jnp.zeros((L_CMP, 2 * NSA_DIM, 2 * NSA_DIM), w_cmp.dtype)
    return jnp.concatenate([top, bot, zz], axis=1).reshape(L_CMP * 4 * NSA_DIM, 2 * NSA_DIM).astype(BF16)


def compress_blocks(nsa_rows, wbig, *, tk=2048):
    nblk, kk = nsa_rows.shape
    tk = min(tk, kk)
    return pl.pallas_call(
        _compress_body,
        grid=(kk // tk,),
        in_specs=[pl.BlockSpec((nblk, tk), lambda k: (0, k)), pl.BlockSpec((tk, LANES), lambda k: (k, 0))],
        out_specs=pl.BlockSpec((nblk, LANES), lambda k: (0, 0)),
        out_shape=jax.ShapeDtypeStruct((nblk, LANES), F32),
        compiler_params=_cparams(("arbitrary",)),
        name="compress_blocks",
    )(nsa_rows, wbig)


def _head_slabs(qf, lane):
    slabs = []
    for p in range(H_NSA // 2):
        x = qf[:, p * LANES:(p + 1) * LANES]
        slabs.append(jnp.where(lane < 64, x, 0.0))
        slabs.append(jnp.where(lane < 64, pltpu.roll(x, 64, 1), 0.0))
    return jnp.concatenate(slabs, axis=0)


def _select_blocks(imp, valid, nb, n_sel):
    bil = lax.broadcasted_iota(jnp.int32, imp.shape, 1)
    rank = jnp.zeros(imp.shape, F32)
    for j in range(nb):
        col = imp[:, j:j + 1]
        ahead = (col > imp) | ((col == imp) & (bil > j))
        rank = rank + jnp.where(ahead, 1.0, 0.0)
    return jnp.where((rank < n_sel) & valid, 1.0, 0.0)


def _nsa_prompt_body(tq, t, nb, span, n_sel, q_ref, ckv_ref, kv_ref, win_ref, misc_ref, e_ref, o_ref):
    i = pl.program_id(1)
    lane = lax.broadcasted_iota(jnp.int32, (tq, LANES), 1)
    q = _head_slabs(q_ref[0].astype(F32), lane).astype(BF16)
    rows = H_NSA * tq
    row4 = lax.broadcasted_iota(jnp.int32, (rows, LANES), 0)
    qpos4 = i * tq + (row4 & (tq - 1))
    bi4 = lax.broadcasted_iota(jnp.int32, (rows, LANES), 1)
    ckv16 = ckv_ref[0].astype(BF16)
    done = ((bi4 + 1) * L_CMP - 1 <= qpos4) & (bi4 < nb)
    pc = _softmax_rows(jnp.where(done, _dot_nt(q, ckv16), -jnp.inf))
    oc = _dot(pc.astype(BF16), ckv16)
    imp = pc[0:tq] + pc[tq:2 * tq] + pc[2 * tq:3 * tq] + pc[3 * tq:4 * tq]
    qpos1 = i * tq + lax.broadcasted_iota(jnp.int32, (tq, LANES), 0)
    cur = qpos1 // L_CMP
    forced = (lane == 0) | (lane == cur) | (lane == cur - 1)
    valid = (lane <= cur) & (lane < nb)
    imp = jnp.where(valid, jnp.where(forced, FORCED_SCORE, imp), -jnp.inf)
    sel = _select_blocks(imp, valid, nb, n_sel)
    seltok = _dot(sel.astype(BF16), e_ref[...])
    kpos = lax.broadcasted_iota(jnp.int32, (tq, t), 1)
    qpos_t = i * tq + lax.broadcasted_iota(jnp.int32, (tq, t), 0)
    bias = jnp.where((seltok > 0.5) & (kpos <= qpos_t), 0.0, -jnp.inf)
    kv16 = kv_ref[0].astype(BF16)
    ps = _softmax_rows(_dot_nt(q, kv16) + jnp.concatenate([bias] * H_NSA, axis=0))
    osel = _dot(ps.astype(BF16), kv16)
    start = jnp.clip(i * tq - WINDOW, 0, t - span)
    start = pl.multiple_of(start, 8)
    w16 = win_ref[0, pl.ds(start, span), :].astype(BF16)
    roww = lax.broadcasted_iota(jnp.int32, (rows, span), 0)
    diff = i * tq + (roww & (tq - 1)) - (start + lax.broadcasted_iota(jnp.int32, (rows, span), 1))
    pw = _softmax_rows(jnp.where((diff >= 0) & (diff <= WINDOW), _dot_nt(q, w16), -jnp.inf))
    ow = _dot(pw.astype(BF16), w16)
    g = misc_ref[0]
    heads = []
    for h in range(H_NSA):
        sl = slice(h * tq, (h + 1) * tq)
        c0 = MISC_GATE + 3 * h
        heads.append(g[:, c0:c0 + 1] * oc[sl] + g[:, c0 + 1:c0 + 2] * osel[sl] + g[:, c0 + 2:c0 + 3] * ow[sl])
    pairs = [jnp.where(lane < 64, pltpu.roll(heads[2 * p], 64, 1), heads[2 * p + 1]) for p in range(H_NSA // 2)]
    o_ref[0] = jnp.concatenate(pairs, axis=1).astype(o_ref.dtype)


def nsa_prompt(nq, cmpkv, nsa, win, misc, *, tq=128):
    b, t, _ = nq.shape
    nb = -(-t // L_CMP)
    tq = min(tq, t)
    span = min(t, WINDOW + tq)
    n_sel = min(N_SEL, nb)
    e = (jnp.arange(t)[None, :] // L_CMP == jnp.arange(LANES)[:, None]).astype(BF16)
    return pl.pallas_call(
        functools.partial(_nsa_prompt_body, tq, t, nb, span, n_sel),
        grid=(b, t // tq),
        in_specs=[pl.BlockSpec((1, tq, 256), lambda bb, i: (bb, i, 0)),
                  pl.BlockSpec((1, LANES, LANES), lambda bb, i: (bb, 0, 0)),
                  pl.BlockSpec((1, t, LANES), lambda bb, i: (bb, 0, 1)),
                  pl.BlockSpec((1, t, LANES), lambda bb, i: (bb, 0, 0)),
                  pl.BlockSpec((1, tq, LANES), lambda bb, i: (bb, i, 0)),
                  pl.BlockSpec((LANES, t), lambda bb, i: (0, 0))],
        out_specs=pl.BlockSpec((1, tq, 256), lambda bb, i: (bb, i, 0)),
        out_shape=jax.ShapeDtypeStruct((b, t, 256), BF16),
        compiler_params=_cparams(("arbitrary", "arbitrary")),
        name="nsa_prompt",
    )(nq, cmpkv, nsa, win, misc, e)


def _outproj_body(decode, *refs):
    if decode:
        ol_ref, wuvb_ref, on_ref, of_ref, w_ref, h_ref, m_ref, gpost_ref, o_ref = refs
        om = _dot(ol_ref[...].astype(BF16), wuvb_ref[...]).astype(BF16)
    else:
        om_ref, on_ref, of_ref, w_ref, h_ref, m_ref, gpost_ref, o_ref = refs
        om = om_ref[...]
    g_, r_, d_ = h_ref.shape
    o = jnp.concatenate([om, on_ref[...].astype(BF16), of_ref[...].astype(BF16)], axis=1)
    yn = _rms(_dot(o, w_ref[...]), gpost_ref[...]).reshape(g_, r_, d_)
    o_ref[...] = h_ref[...] + m_ref[2] * yn


def out_proj(o_mla, o_nsa, o_fox, w_out, h, mods, g_post, *, wuv_bd=None, tm_cap=512):
    b, t, d = h.shape
    decode = wuv_bd is not None
    if not decode:
        tm = _tok_tile(t, tm_cap)
        tpb = t // tm
        blk = (1, tm, d)
        hmap = lambda i: (i // tpb, i % tpb, 0)
        mmap = lambda i: (1, i // tpb, 0, 0)
        grid0, mblk = b * tpb, (3, 1, 1, d)
    else:
        gb = min(b, max(1, tm_cap // t))
        tm = gb * t
        blk = (gb, t, d)
        hmap = lambda i: (i, 0, 0)
        mmap = lambda i: (1, i, 0, 0)
        grid0, mblk = b // gb, (3, gb, 1, d)
    row = lambda a: pl.BlockSpec((tm, a.shape[1]), lambda i: (i, 0))
    full = lambda a: pl.BlockSpec(a.shape, lambda i: (0,) * a.ndim)
    ins = [o_mla] + ([wuv_bd] if decode else []) + [o_nsa, o_fox, w_out, h, mods, g_post.reshape(1, d)]
    specs = [row(o_mla)] + ([full(wuv_bd)] if decode else []) + \
            [row(o_nsa), row(o_fox), full(w_out), pl.BlockSpec(blk, hmap), pl.BlockSpec(mblk, mmap),
             pl.BlockSpec((1, d), lambda i: (0, 0))]
    return pl.pallas_call(
        functools.partial(_outproj_body, decode),
        grid=(grid0,),
        in_specs=specs,
        out_specs=pl.BlockSpec(blk, hmap),
        out_shape=jax.ShapeDtypeStruct(h.shape, F32),
        compiler_params=_cparams(("arbitrary",)),
        name="out_proj_dec" if decode else "out_proj",
    )(*ins)


NEW_PAD = LANES


def _page_copy(cache_ref, layer, page, buf, slot, j, rows, sem, lanes=None):
    src = cache_ref.at[layer, page]
    if lanes is not None:
        src = src.at[:, pl.ds(lanes[0], lanes[1])]
    return pltpu.make_async_copy(src, buf.at[slot, pl.ds(j * rows, rows)], sem.at[slot])


def _fetch_pages(pt_ref, bb, slot, layer, n_pages, rows, pairs):
    def body(j, c):
        page = pt_ref[bb, j]
        for cache_ref, buf, sem, lanes in pairs:
            _page_copy(cache_ref, layer, page, buf, slot, j, rows, sem, lanes).start()
        return c

    lax.fori_loop(0, n_pages, body, 0)


def _wait_pages(slot, layer, n_pages, rows, pairs):
    def body(j, c):
        for cache_ref, buf, sem, lanes in pairs:
            _page_copy(cache_ref, layer, 0, buf, slot, j, rows, sem, lanes).wait()
        return c

    lax.fori_loop(0, n_pages, body, 0)


def _gather_step(pt_ref, layer, n_pages, rows, pairs):
    b = pl.program_id(0)
    slot = b % 2

    @pl.when(b == 0)
    def _():
        _fetch_pages(pt_ref, 0, 0, layer, n_pages, rows, pairs)

    @pl.when(b + 1 < pl.num_programs(0))
    def _():
        _fetch_pages(pt_ref, b + 1, 1 - slot, layer, n_pages, rows, pairs)

    _wait_pages(slot, layer, n_pages, rows, pairs)
    return slot


def _pad_new(x):
    dt, w = x.shape
    return jnp.concatenate([x, jnp.zeros((NEW_PAD - dt, w), x.dtype)], axis=0).astype(BF16)


def _new_bias(rows, dt, ok=None):
    r = lax.broadcasted_iota(jnp.int32, (rows, NEW_PAD), 0)
    c = lax.broadcasted_iota(jnp.int32, (rows, NEW_PAD), 1)
    vis = c <= (r % dt)
    if ok is not None:
        vis = vis & ok
    return jnp.where(vis, 0.0, -jnp.inf)


def _joint_softmax_pv(sa, sb, va, vb):
    m = jnp.maximum(jnp.max(sa, axis=-1, keepdims=True), jnp.max(sb, axis=-1, keepdims=True))
    m = jnp.where(m > -jnp.inf, m, 0.0)
    ea, eb = jnp.exp(sa - m), jnp.exp(sb - m)
    d = jnp.sum(ea, axis=-1, keepdims=True) + jnp.sum(eb, axis=-1, keepdims=True)
    o = _dot(ea.astype(BF16), va) + _dot(eb.astype(BF16), vb)
    return o * (1.0 / jnp.where(d > 0, d, 1.0))


def _mla_dec_body(layer, n_pages, page, pt_ref, qlat_ref, qx_ref, ckvn_ref, misc_ref, ckv_hbm, kr_hbm, o_ref,
                  ckv_buf, kr_buf, sem_c, sem_r):
    dt = qlat_ref.shape[1]
    slot = _gather_step(pt_ref, layer, n_pages, page, [(ckv_hbm, ckv_buf, sem_c, None), (kr_hbm, kr_buf, sem_r, None)])
    ql = qlat_ref[0]
    q = jnp.concatenate([ql[:, h * KV_LORA:(h + 1) * KV_LORA] for h in range(H_MLA)], axis=0).astype(BF16)
    lane = lax.broadcasted_iota(jnp.int32, (dt, LANES), 1)
    qxf = qx_ref[0].astype(F32)
    parts = []
    for h in range(H_MLA):
        x = qxf[:, (h // 2) * LANES:(h // 2 + 1) * LANES]
        if h % 2:
            x = pltpu.roll(x, LANES - ROPE_DIM, 1)
        parts.append(jnp.where(lane < ROPE_DIM, x, 0.0))
    qr = jnp.concatenate(parts, axis=0)[:, :ROPE_DIM].astype(BF16)
    ckv16 = ckv_buf[slot].astype(BF16)
    kr16 = kr_buf[slot].astype(BF16)
    ckvn16 = _pad_new(ckvn_ref[0])
    krn16 = _pad_new(misc_ref[0][:, :ROPE_DIM])
    sa = _dot_nt(q, ckv16) + _dot_nt(qr, kr16)
    sb = _dot_nt(q, ckvn16) + _dot_nt(qr, krn16) + _new_bias(H_MLA * dt, dt)
    o = _joint_softmax_pv(sa, sb, ckv16, ckvn16)
    for h in range(H_MLA):
        o_ref[0, :, h * KV_LORA:(h + 1) * KV_LORA] = o[h * dt:(h + 1) * dt].astype(o_ref.dtype)


def mla_decode(layer, page_table, qlat, qx, ckv_new, misc, cache_ckv, cache_kr):
    db, dt, _ = qlat.shape
    n_pages, page = page_table.shape[1], cache_ckv.shape[2]
    past = n_pages * page
    blk = lambda w: pl.BlockSpec((1, dt, w), lambda b, pt: (b, 0, 0))
    any_spec = pl.BlockSpec(memory_space=pl.ANY)
    return pl.pallas_call(
        functools.partial(_mla_dec_body, layer, n_pages, page),
        grid_spec=pltpu.PrefetchScalarGridSpec(
            num_scalar_prefetch=1, grid=(db,),
            in_specs=[blk(H_MLA * KV_LORA), blk(qx.shape[-1]), blk(KV_LORA), blk(LANES), any_spec, any_spec],
            out_specs=blk(H_MLA * KV_LORA),
            scratch_shapes=[pltpu.VMEM((2, past, KV_LORA), F32), pltpu.VMEM((2, past, ROPE_DIM), F32),
                            pltpu.SemaphoreType.DMA((2,)), pltpu.SemaphoreType.DMA((2,))]),
        out_shape=jax.ShapeDtypeStruct((db, dt, H_MLA * KV_LORA), F32),
        compiler_params=_cparams(("arbitrary",)),
        name="mla_decode",
    )(page_table, qlat, qx, ckv_new, misc, cache_ckv, cache_kr)


def _fox_tri():
    rows = LANES
    m = np.zeros((rows * H_FOX, rows * H_FOX + LANES), np.float32)
    for h in range(H_FOX):
        for r2 in range(rows):
            m[r2 * H_FOX + h, h * rows:h * rows + r2] = 1.0
            m[r2 * H_FOX + h, rows * H_FOX + h] = 1.0
    return jnp.asarray(np.concatenate([m, m, m], axis=0), BF16)


def _fox_dec_body(layer, n_pages, page, pt_ref, fq_ref, fnew_ref, fcum_ref, tri_ref, kv_hbm, lf_hbm, o_ref,
                  kv_buf, lf_buf, sem_k, sem_l):
    dt = fq_ref.shape[1]
    rows = H_FOX * dt
    hw = H_FOX * FOX_DIM
    b = pl.program_id(0)
    slot = b % 2

    def lf_copy(bb, sl, j):
        return pltpu.make_async_copy(lf_hbm.at[layer, pl.ds(pt_ref[bb, j], 1)], lf_buf.at[sl, pl.ds(j, 1)],
                                     sem_l.at[sl])

    def fetch(bb, sl):
        _fetch_pages(pt_ref, bb, sl, layer, n_pages, page, [(kv_hbm, kv_buf, sem_k, None)])
        lax.fori_loop(0, n_pages, lambda j, c: (lf_copy(bb, sl, j).start(), c)[1], 0)

    @pl.when(b == 0)
    def _():
        fetch(0, 0)

    @pl.when(b + 1 < pl.num_programs(0))
    def _():
        fetch(b + 1, 1 - slot)

    _wait_pages(slot, layer, n_pages, page, [(kv_hbm, kv_buf, sem_k, None)])
    lax.fori_loop(0, n_pages, lambda j, c: (lf_copy(0, slot, j).wait(), c)[1], 0)

    lane = lax.broadcasted_iota(jnp.int32, (dt, hw), 1)
    qf = fq_ref[0]
    q = jnp.concatenate([jnp.where(lane // FOX_DIM == h, qf, 0.0) for h in range(H_FOX)], axis=0).astype(BF16)
    k16 = kv_buf[slot, :, 0:hw].astype(BF16)
    v16 = kv_buf[slot, :, hw:2 * hw].astype(BF16)
    fnew = fnew_ref[0]
    kn16, vn16 = _pad_new(fnew[:, :hw]), _pad_new(fnew[:, hw:])
    g3 = jnp.concatenate(list(_split3(lf_buf[slot])), axis=1)
    w = _dot(g3, tri_ref[...])
    tot = w[:, page * H_FOX:]
    pj = lax.broadcasted_iota(jnp.int32, (n_pages, n_pages), 0)
    pk = lax.broadcasted_iota(jnp.int32, (n_pages, n_pages), 1)
    upper = jnp.where(pk > pj, 1.0, 0.0).astype(BF16)
    th, tm_, tl = _split3(tot)
    later = _dot(upper, th) + _dot(upper, tm_) + _dot(upper, tl)
    bias_rows = []
    for h in range(H_FOX):
        r_h = w[:, h * page:(h + 1) * page] + later[:, h:h + 1]
        flat = jnp.concatenate([r_h[j:j + 1, :] for j in range(n_pages)], axis=1)
        bias_rows.append(jnp.broadcast_to(flat, (dt, flat.shape[1])))
    fc = fcum_ref[0]
    gt_col = jnp.concatenate([fc[:, MISC_LOGF + h:MISC_LOGF + h + 1] for h in range(H_FOX)], axis=0)
    sa = _dot_nt(q, k16) + jnp.concatenate(bias_rows, axis=0) + gt_col
    lane_r = lax.broadcasted_iota(jnp.int32, (rows, LANES), 1)
    row_r = lax.broadcasted_iota(jnp.int32, (rows, LANES), 0)
    onehot = jnp.where(lane_r == MISC_LOGF + row_r // dt, 1.0, 0.0).astype(BF16)
    fpad = jnp.concatenate([fc, jnp.zeros((NEW_PAD - dt, LANES), F32)], axis=0)
    f3 = _split3(fpad)
    gt_row = _dot_nt(onehot, f3[0]) + _dot_nt(onehot, f3[1]) + _dot_nt(onehot, f3[2])
    sb = _dot_nt(q, kn16) + gt_col - gt_row + _new_bias(rows, dt)
    o = _joint_softmax_pv(sa, sb, v16, vn16)
    out = jnp.zeros((dt, hw), F32)
    for h in range(H_FOX):
        out = out + jnp.where(lane // FOX_DIM == h, o[h * dt:(h + 1) * dt], 0.0)
    o_ref[0] = out.astype(o_ref.dtype)


def fox_decode(layer, page_table, fq, fox_new, fcum, tri, cache_kv, cache_lf):
    db, dt, hw = fq.shape
    n_pages, page = page_table.shape[1], cache_kv.shape[2]
    past = n_pages * page
    depth, pool = cache_kv.shape[:2]
    kv4 = cache_kv.reshape(depth, pool, page, 2 * hw)
    lf3 = cache_lf.reshape(depth, pool, page * H_FOX)
    blk = lambda w: pl.BlockSpec((1, dt, w), lambda b, pt: (b, 0, 0))
    any_spec = pl.BlockSpec(memory_space=pl.ANY)
    return pl.pallas_call(
        functools.partial(_fox_dec_body, layer, n_pages, page),
        grid_spec=pltpu.PrefetchScalarGridSpec(
            num_scalar_prefetch=1, grid=(db,),
            in_specs=[blk(hw), blk(2 * hw), blk(LANES), pl.BlockSpec(tri.shape, lambda b, pt: (0, 0)),
                      any_spec, any_spec],
            out_specs=blk(hw),
            scratch_shapes=[pltpu.VMEM((2, past, 2 * hw), F32), pltpu.VMEM((2, n_pages, page * H_FOX), F32),
                            pltpu.SemaphoreType.DMA((2,)), pltpu.SemaphoreType.DMA((2,))]),
        out_shape=jax.ShapeDtypeStruct((db, dt, hw), F32),
        compiler_params=_cparams(("arbitrary",)),
        name="fox_decode",
    )(page_table, fq, fox_new, fcum, tri, kv4, lf3)


def _cmp_weight_pairs(w_cmp):
    z = jnp.zeros((L_CMP, NSA_DIM, NSA_DIM), w_cmp.dtype)
    top = jnp.concatenate([w_cmp[0], z], axis=2)
    bot = jnp.concatenate([z, w_cmp[1]], axis=2)
    return jnp.concatenate([top, bot], axis=1).reshape(L_CMP * LANES, LANES).astype(BF16)


def _nsa_dec_body(layer, n_pages, page, n_sel, pt_ref, nq_ref, new_ref, wnew_ref, misc_ref, wpast_ref, wl_ref,
                  e_ref, nsa_hbm, o_ref, wout_ref, cmp_buf, sel_buf, xs_scr, sem_c, sem_s):
    dt = nq_ref.shape[1]
    past = n_pages * page
    nbp = past // L_CMP
    nb = nbp + 1
    nbl = e_ref.shape[0]
    rows = H_NSA * dt
    slot = _gather_step(pt_ref, layer, n_pages, page, [(nsa_hbm, cmp_buf, sem_c, (0, LANES)),
                                                         (nsa_hbm, sel_buf, sem_s, (LANES, LANES))])
    lane = lax.broadcasted_iota(jnp.int32, (dt, LANES), 1)
    q = _head_slabs(nq_ref[0].astype(F32), lane).astype(BF16)
    new = new_ref[0]

    for l in range(L_CMP):
        xs_scr[:, l * LANES:(l + 1) * LANES] = cmp_buf[slot, pl.ds(l, nbp, stride=L_CMP), :].astype(BF16)
    summ = _dot(xs_scr[...], wl_ref[...])
    part = jnp.zeros((1, LANES), F32)
    for l in range(dt):
        part = part + _dot(jnp.broadcast_to(new[l:l + 1, :LANES], (8, LANES)).astype(BF16),
                           wl_ref[l * LANES:(l + 1) * LANES, :])[0:1]
    tail_row = lax.broadcasted_iota(jnp.int32, (nbl - nbp, LANES), 0)
    tail = jnp.where(tail_row == 0, jnp.broadcast_to(part, (nbl - nbp, LANES)), 0.0)
    ckv16 = jnp.concatenate([summ, tail], axis=0).astype(BF16)
    bi4 = lax.broadcasted_iota(jnp.int32, (rows, nbl), 1)
    qpos4 = past + lax.broadcasted_iota(jnp.int32, (rows, nbl), 0) % dt
    done = ((bi4 + 1) * L_CMP - 1 <= qpos4) & (bi4 < nb)
    pc = _softmax_rows(jnp.where(done, _dot_nt(q, ckv16), -jnp.inf))
    oc = _dot(pc.astype(BF16), ckv16)
    imp = pc[0:dt]
    for h in range(1, H_NSA):
        imp = imp + pc[h * dt:(h + 1) * dt]
    bil = lax.broadcasted_iota(jnp.int32, (dt, nbl), 1)
    cur = (past + lax.broadcasted_iota(jnp.int32, (dt, nbl), 0)) // L_CMP
    forced = (bil == 0) | (bil == cur) | (bil == cur - 1)
    valid = (bil <= cur) & (bil < nb)
    imp = jnp.where(valid, jnp.where(forced, FORCED_SCORE, imp), -jnp.inf)
    sel = _select_blocks(imp, valid, nb, n_sel)
    seltok = _dot(sel.astype(BF16), e_ref[...])
    bias = jnp.where(seltok > 0.5, 0.0, -jnp.inf)
    kv16 = sel_buf[slot].astype(BF16)
    kvn16 = _pad_new(new[:, LANES:])
    sel_new = jnp.concatenate([sel[:, nbp:nbp + 1]] * H_NSA, axis=0) > 0.5
    sa = _dot_nt(q, kv16) + jnp.concatenate([bias] * H_NSA, axis=0)
    sb = _dot_nt(q, kvn16) + _new_bias(rows, dt, sel_new)
    osel = _joint_softmax_pv(sa, sb, kv16, kvn16)
    wp = wpast_ref[0, 0]
    wn = wnew_ref[0]
    wb = wp.shape[0]
    wp16, wn16 = wp.astype(BF16), _pad_new(wn)
    qpw = past + lax.broadcasted_iota(jnp.int32, (rows, wb), 0) % dt
    kpw = past - wb + lax.broadcasted_iota(jnp.int32, (rows, wb), 1)
    okw = (qpw - kpw <= WINDOW) & (kpw >= 0)
    sa = jnp.where(okw, _dot_nt(q, wp16), -jnp.inf)
    sb = _dot_nt(q, wn16) + _new_bias(rows, dt)
    ow = _joint_softmax_pv(sa, sb, wp16, wn16)
    wout_ref[0] = jnp.concatenate([wp[dt:], wn], axis=0)
    g = misc_ref[0]
    heads = []
    for h in range(H_NSA):
        sl = slice(h * dt, (h + 1) * dt)
        c0 = MISC_GATE + 3 * h
        heads.append(g[:, c0:c0 + 1] * oc[sl] + g[:, c0 + 1:c0 + 2] * osel[sl] + g[:, c0 + 2:c0 + 3] * ow[sl])
    pairs = [jnp.where(lane < 64, pltpu.roll(heads[2 * p], 64, 1), heads[2 * p + 1]) for p in range(H_NSA // 2)]
    o_ref[0] = jnp.concatenate(pairs, axis=1).astype(o_ref.dtype)


def nsa_decode(layer, page_table, nq, nsa_new, win_new, misc, state_win, wl, cache_nsa):
    db, dt, _ = nq.shape
    n_pages, page = page_table.shape[1], cache_nsa.shape[2]
    past = n_pages * page
    assert past % L_CMP == 0 and dt <= min(L_CMP, WINDOW, NEW_PAD)
    depth, pool = cache_nsa.shape[:2]
    wb = state_win.shape[2]
    nb = past // L_CMP + 1
    nbl = -(-nb // LANES) * LANES
    n_sel = min(N_SEL, nb)
    e = (jnp.arange(past)[None, :] // L_CMP == jnp.arange(nbl)[:, None]).astype(BF16)
    nsa4 = cache_nsa.reshape(depth, pool, page, 4 * NSA_DIM)
    win4 = state_win.reshape(depth, db, wb, 2 * NSA_DIM)
    blk = lambda w: pl.BlockSpec((1, dt, w), lambda b, pt: (b, 0, 0))
    return pl.pallas_call(
        functools.partial(_nsa_dec_body, layer, n_pages, page, n_sel),
        grid_spec=pltpu.PrefetchScalarGridSpec(
            num_scalar_prefetch=1, grid=(db,),
            in_specs=[blk(256), blk(256), blk(LANES), blk(LANES),
                      pl.BlockSpec((1, 1, wb, LANES), lambda b, pt: (layer, b, 0, 0)),
                      pl.BlockSpec(wl.shape, lambda b, pt: (0, 0)),
                      pl.BlockSpec(e.shape, lambda b, pt: (0, 0)),
                      pl.BlockSpec(memory_space=pl.ANY)],
            out_specs=[blk(256), pl.BlockSpec((1, wb, LANES), lambda b, pt: (b, 0, 0))],
            scratch_shapes=[pltpu.VMEM((2, past, LANES), F32), pltpu.VMEM((2, past, LANES), F32),
                            pltpu.VMEM((past // L_CMP, L_CMP * LANES), BF16),
                            pltpu.SemaphoreType.DMA((2,)), pltpu.SemaphoreType.DMA((2,))]),
        out_shape=[jax.ShapeDtypeStruct((db, dt, 256), F32), jax.ShapeDtypeStruct((db, wb, LANES), F32)],
        compiler_params=_cparams(("arbitrary",)),
        name="nsa_decode",
    )(page_table, nq, nsa_new, win_new, misc, win4, wl, e, nsa4)


def _mixer_prompt(h, mods, g_pre, g_post, lw, consts):
    b, t, d = h.shape
    z = in_proj(h, mods, g_pre, lw, consts["tab_p"], lw["bvec"], consts["ltri_p"], consts["place"], decode=False)
    r3 = lambda a: a.reshape(b, t, a.shape[-1])
    o_mla = flash_pairs(r3(z["qn"]), r3(z["qx"]), r3(z["kn"]), r3(z["kx"]), r3(z["vm"]),
                        xw=ROPE_DIM, kx_shared=True)
    fkv16 = r3(z["fkv16"])
    o_fox = flash_pairs(r3(z["fq"]), r3(z["fqx"]), fkv16, r3(z["fkx"]), fkv16,
                        xw=FOX_XW, kx_shared=False, v_off=H_FOX // 2)
    nb = t // L_CMP
    cmpkv = compress_blocks(z["nsa"].reshape(b * nb, L_CMP * 256), lw["wbig"]).reshape(b, nb, LANES)
    cmpkv = jnp.pad(cmpkv, ((0, 0), (0, LANES - nb), (0, 0)))
    o_nsa = nsa_prompt(r3(z["nq"]), cmpkv, r3(z["nsa"]), r3(z["win"]), r3(z["misc"]))
    n = b * t
    h_new = out_proj(o_mla.reshape(n, -1), o_nsa.reshape(n, -1), o_fox.reshape(n, -1), lw["w_out"], h, mods, g_post)
    wb = min(WINDOW, t)
    misc = r3(z["misc"])
    state = (r3(z["ckv"]), misc[:, :, :ROPE_DIM], r3(z["nsa"]).reshape(b, t, 4, NSA_DIM),
             r3(z["win"])[:, t - wb:].reshape(b, wb, 2, NSA_DIM),
             r3(z["fox"]).reshape(b, t, 2, H_FOX, FOX_DIM), misc[:, :, MISC_LOGF:MISC_END])
    return h_new, state


def _mixer_sample(h, mods, g_pre, g_post, lw, consts, l, page_table, c_ckv, c_kr, c_nsa, s_win, c_fkv, c_flf):
    db, dt, d = h.shape
    z = in_proj(h, mods, g_pre, lw, consts["tab_s"], lw["bvec"], consts["ltri_s"], consts["place"], decode=True)
    r3 = lambda a: a.reshape(db, dt, a.shape[-1])
    misc = r3(z["misc"])
    o_lat = mla_decode(l, page_table, r3(z["qlat"]), r3(z["qx"]), r3(z["ckv"]), misc, c_ckv, c_kr)
    o_nsa, win_new = nsa_decode(l, page_table, r3(z["nq"]), r3(z["nsa"]), r3(z["win"]), misc, s_win, lw["wl"], c_nsa)
    o_fox = fox_decode(l, page_table, r3(z["fq"]), r3(z["fox"]), r3(z["fcum"]), consts["fox_tri"], c_fkv, c_flf)
    n = db * dt
    h_new = out_proj(o_lat.reshape(n, -1), o_nsa.reshape(n, -1), o_fox.reshape(n, -1), lw["w_out"], h, mods, g_post,
                     wuv_bd=lw["wuv_bd"])
    state = (r3(z["ckv"]), misc[:, :, :ROPE_DIM], r3(z["nsa"]).reshape(db, dt, 4, NSA_DIM),
             win_new.reshape(db, -1, 2, NSA_DIM), r3(z["fox"]).reshape(db, dt, 2, H_FOX, FOX_DIM),
             misc[:, :, MISC_LOGF:MISC_END])
    return h_new, state


def _layer_weights(l, w_in, b_fox_f, mla_g_q, mla_g_kv, mla_w_uq, mla_w_uk, mla_w_uv, nsa_w_cmp, w_out):
    lw = _inproj_weights(w_in[l], mla_w_uq[l], mla_w_uk[l], mla_w_uv[l])
    bvec = jnp.zeros((1, LANES), F32).at[0, MISC_LOGF:MISC_END].set(b_fox_f[l])
    eye = jnp.eye(H_MLA, dtype=F32)
    wuv_bd = jnp.einsum("chd,hg->hcgd", mla_w_uv[l], eye).reshape(H_MLA * KV_LORA, H_MLA * V_DIM)
    lw.update(g_q=mla_g_q[l], g_kv=mla_g_kv[l], bvec=bvec, wbig=_cmp_weight(nsa_w_cmp[l]),
              w_out=w_out[l].astype(BF16), wuv_bd=wuv_bd.astype(BF16), wl=_cmp_weight_pairs(nsa_w_cmp[l]))
    return lw


def kernel(x_prompt, x_sample, cache_mla_ckv, cache_mla_krope, cache_nsa_kv, state_nsa_win, cache_fox_kv,
           cache_fox_logf, page_table, c_prompt, c_sample, ada_w, ada_b, norm_pre, norm_post, ffn_w_gate,
           ffn_w_up, ffn_w_down, w_in, b_fox_f, mla_g_q, mla_g_kv, mla_w_uq, mla_w_uk, mla_w_uv, nsa_w_cmp, w_out):
    depth = w_in.shape[0]
    b, t, d = x_prompt.shape
    db, dt, _ = x_sample.shape
    past = page_table.shape[1] * cache_mla_ckv.shape[2]
    mods_all = ada_mods(jnp.concatenate([c_prompt, c_sample], axis=0), ada_w, ada_b)
    wg16, wu16, wd16 = ffn_w_gate.astype(BF16), ffn_w_up.astype(BF16), ffn_w_down.astype(BF16)
    tm_p = min(t, 512)
    gb = min(db, max(1, 512 // dt))
    tm_s = gb * dt
    ii = jnp.arange(tm_s)
    consts = dict(
        tab_p=_rope_tables(jnp.arange(t, dtype=jnp.int32)),
        tab_s=jnp.tile(_rope_tables(past + jnp.arange(dt, dtype=jnp.int32)), (gb, 1)),
        ltri_p=(jnp.arange(tm_p)[:, None] >= jnp.arange(tm_p)[None, :]).astype(BF16),
        ltri_s=((ii[:, None] >= ii[None, :]) & (ii[:, None] // dt == ii[None, :] // dt)).astype(BF16),
        place=_fox_place(), fox_tri=_fox_tri())
    hp, hs = x_prompt, x_sample
    st_p, st_s = [], []
    for l in range(depth):
        lw = _layer_weights(l, w_in, b_fox_f, mla_g_q, mla_g_kv, mla_w_uq, mla_w_uk, mla_w_uv, nsa_w_cmp, w_out)
        mp = mods_all[l, :, :b].reshape(3 * N_SUB, b, 1, d)
        ms = mods_all[l, :, b:].reshape(3 * N_SUB, db, 1, d)
        hp = ffn_half(hp, mp, 0, norm_pre[l, 0], norm_post[l, 0], wg16, wu16, wd16, l, 0)
        hs = ffn_half(hs, ms, 0, norm_pre[l, 0], norm_post[l, 0], wg16, wu16, wd16, l, 0)
        hp, sp = _mixer_prompt(hp, mp, norm_pre[l, 1], norm_post[l, 1], lw, consts)
        hs, ss = _mixer_sample(hs, ms, norm_pre[l, 1], norm_post[l, 1], lw, consts, l, page_table,
                               cache_mla_ckv, cache_mla_krope, cache_nsa_kv, state_nsa_win, cache_fox_kv,
                               cache_fox_logf)
        hp = ffn_half(hp, mp, 2, norm_pre[l, 2], norm_post[l, 2], wg16, wu16, wd16, l, 1)
        hs = ffn_half(hs, ms, 2, norm_pre[l, 2], norm_post[l, 2], wg16, wu16, wd16, l, 1)
        st_p.append(sp)
        st_s.append(ss)
    outs_p = tuple(jnp.stack(a) for a in zip(*st_p))
    outs_s = tuple(jnp.stack(a) for a in zip(*st_s))
    return (hp, hs) + outs_p + outs_s
```

```python
import functools

import numpy as np
import jax
import jax.numpy as jnp
from jax import lax
from jax.experimental import pallas as pl
from jax.experimental.pallas import tpu as pltpu

F32, BF16 = jnp.float32, jnp.bfloat16

H_MLA, Q_LORA, KV_LORA, NOPE_DIM, ROPE_DIM, V_DIM = 8, 384, 256, 64, 32, 64
H_NSA, NSA_DIM, L_CMP, N_SEL, WINDOW = 4, 64, 64, 16, 512
H_FOX, FOX_DIM = 4, 64
N_SUB = 3
ROPE_THETA = 10000.0
FORCED_SCORE = 1.0e4
EPS = 1e-6
MLA_SCALE = (NOPE_DIM + ROPE_DIM) ** -0.5
NSA_SCALE = NSA_DIM ** -0.5
FOX_SCALE = FOX_DIM ** -0.5
IN_SPLITS = (Q_LORA, KV_LORA, ROPE_DIM, H_NSA * NSA_DIM, 6 * NSA_DIM, H_NSA * 3,
             H_FOX * FOX_DIM, H_FOX * FOX_DIM, H_FOX * FOX_DIM, H_FOX)

LANES = 128
VMEM_LIMIT = 56 * 1024 * 1024

MISC_GATE = ROPE_DIM
MISC_LOGF = ROPE_DIM + H_NSA * 3
MISC_END = MISC_LOGF + H_FOX
FOX_XW = 6

_W_WIDTHS = (("cq", Q_LORA), ("ckv", KV_LORA), ("nq", 256), ("nqs", 256), ("nkv", 384), ("nkvs", 384),
             ("fq", 256), ("fkv", 512), ("misc", 128), ("miscs", 128), ("krp", 128), ("krps", 128))
_W_OFF = {}
_acc = 0
for _n, _w in _W_WIDTHS:
    _W_OFF[_n] = (_acc, _acc + _w)
    _acc += _w
W_TOTAL = _acc
_TAB = {n: (i * LANES, (i + 1) * LANES) for i, n in
        enumerate(("ckx", "skx", "cm", "sm", "cn", "sn", "cnkv", "snkv"))}
TAB_TOTAL = 8 * LANES


def _cparams(sem):
    return pltpu.CompilerParams(dimension_semantics=sem, vmem_limit_bytes=VMEM_LIMIT)


def _rms(x, g):
    ms = jnp.mean(x * x, axis=-1, keepdims=True)
    return x * lax.rsqrt(ms + EPS) * g


def _silu(x):
    return x / (1.0 + jnp.exp(-x))


def _softmax_rows(s):
    m = jnp.max(s, axis=-1, keepdims=True)
    m = jnp.where(m > -jnp.inf, m, 0.0)
    e = jnp.exp(s - m)
    d = jnp.sum(e, axis=-1, keepdims=True)
    return e * (1.0 / jnp.where(d > 0, d, 1.0))


def _dot(a, b):
    return jnp.dot(a, b, preferred_element_type=F32)


def _dot_nt(a, b):
    return lax.dot_general(a, b, (((1,), (1,)), ((), ())), preferred_element_type=F32)


def _split3(x):
    hi = x.astype(BF16)
    r1 = x - hi.astype(F32)
    mid = r1.astype(BF16)
    lo = (r1 - mid.astype(F32)).astype(BF16)
    return hi, mid, lo


def _ada_body(c_ref, w_ref, b_ref, o_ref):
    s = _silu(c_ref[...])
    o_ref[0, 0] = _dot(s.astype(BF16), w_ref[0].astype(BF16)) + b_ref[0, 0]


def ada_mods(c_all, ada_w, ada_b):
    depth, d, n9 = ada_w.shape
    nc = n9 // d
    bt = c_all.shape[0]
    return pl.pallas_call(
        _ada_body,
        grid=(depth, nc),
        in_specs=[pl.BlockSpec((bt, d), lambda l, n: (0, 0)),
                  pl.BlockSpec((1, d, d), lambda l, n: (l, 0, n)),
                  pl.BlockSpec((1, 1, 1, d), lambda l, n: (l, n, 0, 0))],
        out_specs=pl.BlockSpec((1, 1, bt, d), lambda l, n: (l, n, 0, 0)),
        out_shape=jax.ShapeDtypeStruct((depth, nc, bt, d), F32),
        compiler_params=_cparams(("arbitrary", "arbitrary")),
        name="ada_mods",
    )(c_all, ada_w, ada_b.reshape(depth, nc, 1, d))


def _ffn_body(h_ref, m_ref, gpre_ref, gpost_ref, wg_ref, wu_ref, wd_ref, o_ref, u_scr, acc_scr):
    f = pl.program_id(1)
    g_, r_, d_ = h_ref.shape

    @pl.when(f == 0)
    def _():
        y = _rms(h_ref[...], gpre_ref[...])
        u = y * (1.0 + m_ref[1]) + m_ref[0]
        u_scr[...] = u.reshape(g_ * r_, d_).astype(BF16)
        acc_scr[...] = jnp.zeros_like(acc_scr)

    u = u_scr[...]
    gate = _dot(u, wg_ref[0, 0])
    up = _dot(u, wu_ref[0, 0])
    a = (_silu(gate) * up).astype(BF16)
    acc_scr[...] += _dot(a, wd_ref[0, 0])

    @pl.when(f == pl.num_programs(1) - 1)
    def _():
        yn = _rms(acc_scr[...], gpost_ref[...]).reshape(g_, r_, d_)
        o_ref[...] = h_ref[...] + 0.5 * m_ref[2] * yn


def _tok_tile(n_rows_per_group, cap):
    return min(n_rows_per_group, cap)


def ffn_half(h, mods, sub, g_pre, g_post, wg, wu, wd, layer, half, *, tm_cap=1024, tf=256):
    b, t, d = h.shape
    dff = wg.shape[-1]
    tf = min(tf, dff)
    if t >= 8 * 16:
        tm = _tok_tile(t, tm_cap)
        blk, tpb = (1, tm, d), t // tm
        hmap = lambda i, f: (i // tpb, i % tpb, 0)
        mmap = lambda i, f: (sub, i // tpb, 0, 0)
        grid0, mblk = b * tpb, (3, 1, 1, d)
    else:
        gb = min(b, max(1, tm_cap // t))
        blk = (gb, t, d)
        hmap = lambda i, f: (i, 0, 0)
        mmap = lambda i, f: (sub, i, 0, 0)
        grid0, mblk = b // gb, (3, gb, 1, d)
    rows = blk[0] * blk[1]
    return pl.pallas_call(
        _ffn_body,
        grid=(grid0, dff // tf),
        in_specs=[pl.BlockSpec(blk, hmap),
                  pl.BlockSpec(mblk, mmap),
                  pl.BlockSpec((1, d), lambda i, f: (0, 0)),
                  pl.BlockSpec((1, d), lambda i, f: (0, 0)),
                  pl.BlockSpec((1, 1, d, tf), lambda i, f: (layer, half, 0, f)),
                  pl.BlockSpec((1, 1, d, tf), lambda i, f: (layer, half, 0, f)),
                  pl.BlockSpec((1, 1, tf, d), lambda i, f: (layer, half, f, 0))],
        out_specs=pl.BlockSpec(blk, hmap),
        out_shape=jax.ShapeDtypeStruct(h.shape, F32),
        scratch_shapes=[pltpu.VMEM((rows, d), BF16), pltpu.VMEM((rows, d), F32)],
        compiler_params=_cparams(("arbitrary", "arbitrary")),
        name="ffn_half",
    )(h, mods, g_pre.reshape(1, d), g_post.reshape(1, d), wg, wu, wd)


def _swap_halves(w, dh):
    k = w.shape[0]
    w4 = w.reshape(k, w.shape[1] // dh, 2, dh // 2)
    return w4[:, :, ::-1, :].reshape(k, -1)


def _inproj_weights(w_in, w_uq, w_uk, w_uv):
    k = w_in.shape[0]
    cuts = np.cumsum(IN_SPLITS)[:-1].tolist()
    cq, ckv, kr, nq, nkv, ng, fq, fk, fv, ff = jnp.split(w_in, cuts, axis=1)
    z = lambda n: jnp.zeros((k, n), w_in.dtype)
    nkv3 = nkv.reshape(k, 3, 2, NSA_DIM)
    nkvs = jnp.stack([_swap_halves(nkv3[:, :, 0].reshape(k, -1), NSA_DIM).reshape(k, 3, NSA_DIM),
                      jnp.zeros((k, 3, NSA_DIM), w_in.dtype)], axis=2).reshape(k, -1)
    krs = _swap_halves(kr, ROPE_DIM)
    cols = dict(cq=cq, ckv=ckv, nq=nq, nqs=_swap_halves(nq, NSA_DIM), nkv=nkv, nkvs=nkvs, fq=fq,
                fkv=jnp.concatenate([fk, fv], axis=1),
                misc=jnp.concatenate([kr, ng, ff, z(LANES - MISC_END)], axis=1),
                miscs=jnp.concatenate([krs, z(LANES - ROPE_DIM)], axis=1),
                krp=jnp.concatenate([kr, kr, z(LANES - 2 * ROPE_DIM)], axis=1),
                krps=jnp.concatenate([krs, krs, z(LANES - 2 * ROPE_DIM)], axis=1))
    w_all = jnp.concatenate([cols[n] for n, _ in _W_WIDTHS], axis=1).astype(BF16)
    dq = NOPE_DIM + ROPE_DIM
    uq = w_uq.reshape(Q_LORA, H_MLA, dq)
    wqn = uq[:, :, :NOPE_DIM].reshape(Q_LORA, -1)
    rope = uq[:, :, NOPE_DIM:]
    zr = jnp.zeros((Q_LORA, H_MLA // 2, LANES - 2 * ROPE_DIM), w_uq.dtype)

    def pairs(r):
        return jnp.concatenate([r.reshape(Q_LORA, H_MLA // 2, 2 * ROPE_DIM), zr], axis=2).reshape(Q_LORA, -1)

    wqr = pairs(rope)
    wqs = pairs(_swap_halves(rope.reshape(Q_LORA, -1), ROPE_DIM).reshape(Q_LORA, H_MLA, ROPE_DIM))
    wuk = w_uk.reshape(KV_LORA, -1)
    wuv = w_uv.reshape(KV_LORA, -1)
    ukt = jnp.transpose(w_uk, (1, 2, 0))
    zt = jnp.zeros_like(ukt)
    even = jnp.concatenate([ukt, zt], axis=1)
    odd = jnp.concatenate([zt, ukt], axis=1)
    wukt = jnp.where((jnp.arange(H_MLA) % 2 == 0)[:, None, None], even, odd)
    return dict(w_all=w_all, wqn=wqn.astype(BF16), wqr=wqr.astype(BF16), wqs=wqs.astype(BF16),
                wuk=wuk.astype(BF16), wuv=wuv.astype(BF16), wukt=wukt.astype(BF16))


def _rope_tables(pos):
    def cs(dh):
        half = dh // 2
        inv = jnp.power(ROPE_THETA, -jnp.arange(half, dtype=F32) / half)
        ang = pos.astype(F32)[:, None] * inv[None, :]
        c, s = jnp.cos(ang), jnp.sin(ang)
        return jnp.concatenate([c, c], 1), jnp.concatenate([-s, s], 1)

    n = pos.shape[0]
    c32, s32 = cs(ROPE_DIM)
    c64, s64 = cs(NSA_DIM)
    one, zero = jnp.ones((n, 1), F32), jnp.zeros((n, 1), F32)
    rep = lambda a, k: jnp.tile(a, (1, k))
    tabs = dict(
        ckx=jnp.concatenate([c32, c32, rep(zero, 64)], 1), skx=jnp.concatenate([s32, s32, rep(zero, 64)], 1),
        cm=jnp.concatenate([c32, rep(one, 96)], 1), sm=jnp.concatenate([s32, rep(zero, 96)], 1),
        cn=NSA_SCALE * jnp.concatenate([c64, c64], 1), sn=NSA_SCALE * jnp.concatenate([s64, s64], 1),
        cnkv=jnp.concatenate([c64, rep(one, 64)], 1), snkv=jnp.concatenate([s64, rep(zero, 64)], 1))
    return jnp.concatenate([tabs[k] for k in _TAB], axis=1)


def _fox_place():
    pq = np.zeros((3 * LANES, 2 * LANES), np.float32)
    pk = np.zeros((3 * LANES, 2 * LANES), np.float32)
    cq = np.zeros((1, 2 * LANES), np.float32)
    ck = np.zeros((1, 2 * LANES), np.float32)
    for h in range(H_FOX):
        base = (h // 2) * LANES + (h % 2) * FOX_XW
        for s in range(3):
            pq[s * LANES + MISC_LOGF + h, base + s] = 1.0
            pk[s * LANES + MISC_LOGF + h, base + 3 + s] = -1.0
            cq[0, base + 3 + s] = 1.0
            ck[0, base + s] = 1.0
    return (jnp.asarray(pq, BF16), jnp.asarray(pk, BF16), jnp.asarray(cq), jnp.asarray(ck))


def _inproj_body(decode, tpb, *refs):
    (h_ref, m_ref, gpre_ref, w_ref, gq_ref, gkv_ref, wqn_ref, wqr_ref, wqs_ref, wuk_ref, wuv_ref, wukt_ref,
     tab_ref, bvec_ref, ltri_ref, pq_ref, pk_ref, cq_ref, ck_ref) = refs[:19]
    outs = refs[19:-1]
    carry_scr = refs[-1]
    (ckv_o, nsa_o, win_o, fox_o, misc_o, qn_o, qx_o, kx_o, nq_o, fq_o, fkv16_o) = outs[:11]
    g_, r_, d_ = h_ref.shape
    tm = g_ * r_
    y = _rms(h_ref[...], gpre_ref[...])
    u = (y * (1.0 + m_ref[1]) + m_ref[0]).reshape(tm, d_).astype(BF16)

    def z(name):
        a, b = _W_OFF[name]
        return _dot(u, w_ref[:, a:b])

    def tab(name, k=1):
        a, b = _TAB[name]
        t = tab_ref[:, a:b]
        return t if k == 1 else jnp.concatenate([t] * k, axis=1)

    cqn = _rms(z("cq"), gq_ref[...]).astype(BF16)
    qn = _dot(cqn, wqn_ref[...]) * MLA_SCALE
    qn16 = qn.astype(BF16)
    qn_o[...] = qn16.astype(qn_o.dtype)
    qx = (_dot(cqn, wqr_ref[...]) * tab("ckx", 4) + _dot(cqn, wqs_ref[...]) * tab("skx", 4)) * MLA_SCALE
    qx_o[...] = qx.astype(qx_o.dtype)
    ckvn = _rms(z("ckv"), gkv_ref[...])
    ckv_o[...] = ckvn
    kx_o[...] = (z("krp") * tab("ckx") + z("krps") * tab("skx")).astype(kx_o.dtype)
    if decode:
        qlat_o, fcum_o = outs[11:]
        for h in range(H_MLA):
            p = h // 2
            qlat_o[:, h * KV_LORA:(h + 1) * KV_LORA] = _dot(qn16[:, p * LANES:(p + 1) * LANES], wukt_ref[h])
    else:
        kn_o, vm_o, fqx_o, fkx_o = outs[11:]
        ck16 = ckvn.astype(BF16)
        kn_o[...] = _dot(ck16, wuk_ref[...]).astype(BF16)
        vm_o[...] = _dot(ck16, wuv_ref[...]).astype(BF16)
    nq_o[...] = (z("nq") * tab("cn", 2) + z("nqs") * tab("sn", 2)).astype(nq_o.dtype)
    nkv = z("nkv") * tab("cnkv", 3) + z("nkvs") * tab("snkv", 3)
    nsa_o[...] = nkv[:, :4 * NSA_DIM]
    win_o[...] = nkv[:, 4 * NSA_DIM:]
    fq_o[...] = (z("fq") * FOX_SCALE).astype(fq_o.dtype)
    fkv = z("fkv")
    fox_o[...] = fkv
    fkv16_o[...] = fkv.astype(fkv16_o.dtype)
    zm = z("misc")
    lane = lax.broadcasted_iota(jnp.int32, (tm, LANES), 1)
    roped = zm * tab("cm") + z("miscs") * tab("sm")
    xb = zm + bvec_ref[...]
    logsig = jnp.minimum(xb, 0.0) - jnp.log(1.0 + jnp.exp(-jnp.abs(xb)))
    misc = jnp.where(lane < MISC_GATE, roped,
                     jnp.where(lane < MISC_LOGF, 1.0 / (1.0 + jnp.exp(-zm)),
                               jnp.where(lane < MISC_END, logsig, 0.0)))
    misc_o[...] = misc
    lf = jnp.where((lane >= MISC_LOGF) & (lane < MISC_END), misc, 0.0)
    ltri = ltri_ref[...]
    hi, mid, lo = _split3(lf)
    cum = _dot(ltri, hi) + _dot(ltri, mid) + _dot(ltri, lo)
    first = (pl.program_id(0) % tpb) == 0
    carry = jnp.where(first, 0.0, carry_scr[0:1, :])
    fc = cum + carry
    carry_scr[0:1, :] = fc[tm - 1:tm, :]
    if decode:
        fcum_o[...] = fc
    else:
        g3 = jnp.concatenate(list(_split3(fc)), axis=1)
        fqx_o[...] = (_dot(g3, pq_ref[...]) + cq_ref[...]).astype(BF16)
        fkx_o[...] = (_dot(g3, pk_ref[...]) + ck_ref[...]).astype(BF16)


def in_proj(h, mods, g_pre, lw, tables, bvec, ltri, place, *, decode, tm_cap=512):
    b, t, d = h.shape
    n = b * t
    if not decode:
        tm = _tok_tile(t, tm_cap)
        tpb = t // tm
        blk = (1, tm, d)
        hmap = lambda i: (i // tpb, i % tpb, 0)
        mmap = lambda i: (1, i // tpb, 0, 0)
        tmap = lambda i: (i % tpb, 0)
        grid0, mblk = b * tpb, (3, 1, 1, d)
    else:
        gb = min(b, max(1, tm_cap // t))
        tm, tpb = gb * t, 1
        blk = (gb, t, d)
        hmap = lambda i: (i, 0, 0)
        mmap = lambda i: (1, i, 0, 0)
        tmap = lambda i: (0, 0)
        grid0, mblk = b // gb, (3, gb, 1, d)
    full = lambda a: pl.BlockSpec(a.shape, lambda i: (0,) * a.ndim)
    pq, pk, cq, ck = place
    ins = [h, mods, g_pre.reshape(1, d), lw["w_all"], lw["g_q"].reshape(1, -1), lw["g_kv"].reshape(1, -1),
           lw["wqn"], lw["wqr"], lw["wqs"], lw["wuk"], lw["wuv"], lw["wukt"], tables, bvec, ltri, pq, pk, cq, ck]
    in_specs = [pl.BlockSpec(blk, hmap), pl.BlockSpec(mblk, mmap)] + [full(a) for a in ins[2:12]] + \
               [pl.BlockSpec((tm, TAB_TOTAL), tmap)] + [full(a) for a in ins[13:]]
    names = ["ckv", "nsa", "win", "fox", "misc", "qn", "qx", "kx", "nq", "fq", "fkv16"]
    widths = [KV_LORA, 256, 128, 512, 128, 512, 512, 128, 256, 256, 512]
    dts = [F32] * 5 + [F32 if decode else BF16] * 6
    if decode:
        names += ["qlat", "fcum"]
        widths += [H_MLA * KV_LORA, 128]
        dts += [F32, F32]
    else:
        names += ["kn", "vm", "fqx", "fkx"]
        widths += [512, 512, 256, 256]
        dts += [BF16, BF16, BF16, BF16]
    outs = pl.pallas_call(
        functools.partial(_inproj_body, decode, tpb),
        grid=(grid0,),
        in_specs=in_specs,
        out_specs=[pl.BlockSpec((tm, w), lambda i: (i, 0)) for w in widths],
        out_shape=[jax.ShapeDtypeStruct((n, w), dt) for w, dt in zip(widths, dts)],
        scratch_shapes=[pltpu.VMEM((8, LANES), F32)],
        compiler_params=_cparams(("arbitrary",)),
        name="in_proj_dec" if decode else "in_proj",
    )(*ins)
    return dict(zip(names, outs))


def _flash_body(xw, tq, tk, qa_ref, qx_ref, ka_ref, kx_ref, v_ref, o_ref, m_scr, l_scr, acc_scr):
    i = pl.program_id(2)
    lane = lax.broadcasted_iota(jnp.int32, (tq, LANES), 1)
    qa, qx = qa_ref[0], qx_ref[0]
    zero = jnp.zeros_like(qa)
    q0 = jnp.concatenate([jnp.where(lane < 64, qa, zero), jnp.where(lane < xw, qx, zero)], axis=1)
    q1 = jnp.concatenate([jnp.where(lane >= 64, qa, zero),
                          jnp.where((lane >= xw) & (lane < 2 * xw), qx, zero)], axis=1)
    q2 = jnp.concatenate([q0, q1], axis=0)
    m_scr[...] = jnp.full_like(m_scr, -jnp.inf)
    l_scr[...] = jnp.zeros_like(l_scr)
    acc_scr[...] = jnp.zeros_like(acc_scr)

    def chunk(c, masked):
        k0 = pl.multiple_of(c * tk, tk)
        k2 = jnp.concatenate([ka_ref[0, pl.ds(k0, tk), :], kx_ref[0, pl.ds(k0, tk), :]], axis=1)
        v = v_ref[0, pl.ds(k0, tk), :]
        if masked:
            row = lax.broadcasted_iota(jnp.int32, (tq, tk), 0)
            col = lax.broadcasted_iota(jnp.int32, (tq, tk), 1)
            vis = k0 + col <= i * tq + row
        for hh in range(2):
            rs = slice(hh * tq, (hh + 1) * tq)
            s = _dot_nt(q2[rs], k2)
            if masked:
                s = jnp.where(vis, s, -jnp.inf)
            m_prev = m_scr[rs]
            m_new = jnp.maximum(m_prev, jnp.max(s, axis=1, keepdims=True))
            alpha = jnp.exp(m_prev - m_new)
            p = jnp.exp(s - m_new[:, :1])
            l_scr[rs] = alpha * l_scr[rs] + jnp.sum(p, axis=1, keepdims=True)
            acc_scr[rs] = alpha * acc_scr[rs] + _dot(p.astype(BF16), v)
            m_scr[rs] = m_new

    n_full = (i * tq) // tk

    def body(c, carry):
        chunk(c, False)
        return carry

    lax.fori_loop(0, n_full, body, 0)
    chunk(n_full, True)
    o = acc_scr[...] * (1.0 / l_scr[...])
    o_ref[0] = jnp.where(lane < 64, o[:tq], o[tq:]).astype(o_ref.dtype)


def flash_pairs(qa, qx, ka, kx, v, *, xw, kx_shared, v_off=0, tq=512, tk=512):
    b, t, w = qa.shape
    npair = w // LANES
    tq, tk = min(tq, t), min(tk, t)
    kxmap = (lambda bb, p, i: (bb, 0, 0)) if kx_shared else (lambda bb, p, i: (bb, 0, p))
    return pl.pallas_call(
        functools.partial(_flash_body, xw, tq, tk),
        grid=(b, npair, t // tq),
        in_specs=[pl.BlockSpec((1, tq, LANES), lambda bb, p, i: (bb, i, p)),
                  pl.BlockSpec((1, tq, LANES), lambda bb, p, i: (bb, i, p)),
                  pl.BlockSpec((1, t, LANES), lambda bb, p, i: (bb, 0, p)),
                  pl.BlockSpec((1, t, LANES), kxmap),
                  pl.BlockSpec((1, t, LANES), lambda bb, p, i: (bb, 0, v_off + p))],
        out_specs=pl.BlockSpec((1, tq, LANES), lambda bb, p, i: (bb, i, p)),
        out_shape=jax.ShapeDtypeStruct((b, t, w), BF16),
        scratch_shapes=[pltpu.VMEM((2 * tq, LANES), F32)] * 3,
        compiler_params=_cparams(("arbitrary", "arbitrary", "arbitrary")),
        name="flash_pairs",
    )(qa, qx, ka, kx, v)


def _compress_body(x_ref, w_ref, o_ref):
    @pl.when(pl.program_id(0) == 0)
    def _():
        o_ref[...] = jnp.zeros_like(o_ref)

    o_ref[...] += _dot(x_ref[...].astype(BF16), w_ref[...])


def _cmp_weight(w_cmp):
    z = jnp.zeros((L_CMP, NSA_DIM, NSA_DIM), w_cmp.dtype)
    top = jnp.concatenate([w_cmp[0], z], axis=2)
    bot = jnp.concatenate([z, w_cmp[1]], axis=2)
    zz = jnp.zeros((L_CMP, 2 * NSA_DIM, 2 * NSA_DIM), w_cmp.dtype)
    return jnp.concatenate([top, bot, zz], axis=1).reshape(L_CMP * 4 * NSA_DIM, 2 * NSA_DIM).astype(BF16)


def compress_blocks(nsa_rows, wbig, *, tk=2048):
    nblk, kk = nsa_rows.shape
    tk = min(tk, kk)
    return pl.pallas_call(
        _compress_body,
        grid=(kk // tk,),
        in_specs=[pl.BlockSpec((nblk, tk), lambda k: (0, k)), pl.BlockSpec((tk, LANES), lambda k: (k, 0))],
        out_specs=pl.BlockSpec((nblk, LANES), lambda k: (0, 0)),
        out_shape=jax.ShapeDtypeStruct((nblk, LANES), F32),
        compiler_params=_cparams(("arbitrary",)),
        name="compress_blocks",
    )(nsa_rows, wbig)


def _head_slabs(qf, lane):
    slabs = []
    for p in range(H_NSA // 2):
        x = qf[:, p * LANES:(p + 1) * LANES]
        slabs.append(jnp.where(lane < 64, x, 0.0))
        slabs.append(jnp.where(lane < 64, pltpu.roll(x, 64, 1), 0.0))
    return jnp.concatenate(slabs, axis=0)


def _select_blocks(imp, valid, bid, lane_bids, n_sel):
    rank = jnp.zeros(imp.shape, F32)
    for j, bj in lane_bids:
        col = imp[:, j:j + 1]
        ahead = (col > imp) | ((col == imp) & (bid > bj))
        rank = rank + jnp.where(ahead, 1.0, 0.0)
    return jnp.where((rank < n_sel) & valid, 1.0, 0.0)


def _nsa_prompt_body(tq, t, nb, span, n_sel, q_ref, ckv_ref, kv_ref, win_ref, misc_ref, e_ref, o_ref):
    i = pl.program_id(1)
    lane = lax.broadcasted_iota(jnp.int32, (tq, LANES), 1)
    q = _head_slabs(q_ref[0].astype(F32), lane).astype(BF16)
    rows = H_NSA * tq
    row4 = lax.broadcasted_iota(jnp.int32, (rows, LANES), 0)
    qpos4 = i * tq + (row4 & (tq - 1))
    bi4 = lax.broadcasted_iota(jnp.int32, (rows, LANES), 1)
    ckv16 = ckv_ref[0].astype(BF16)
    done = ((bi4 + 1) * L_CMP - 1 <= qpos4) & (bi4 < nb)
    pc = _softmax_rows(jnp.where(done, _dot_nt(q, ckv16), -jnp.inf))
    oc = _dot(pc.astype(BF16), ckv16)
    imp = pc[0:tq] + pc[tq:2 * tq] + pc[2 * tq:3 * tq] + pc[3 * tq:4 * tq]
    qpos1 = i * tq + lax.broadcasted_iota(jnp.int32, (tq, LANES), 0)
    cur = qpos1 // L_CMP
    forced = (lane == 0) | (lane == cur) | (lane == cur - 1)
    valid = (lane <= cur) & (lane < nb)
    imp = jnp.where(valid, jnp.where(forced, FORCED_SCORE, imp), -jnp.inf)
    sel = _select_blocks(imp, valid, lane, tuple((j, j) for j in range(nb)), n_sel)
    seltok = _dot(sel.astype(BF16), e_ref[...])
    kpos = lax.broadcasted_iota(jnp.int32, (tq, t), 1)
    qpos_t = i * tq + lax.broadcasted_iota(jnp.int32, (tq, t), 0)
    bias = jnp.where((seltok > 0.5) & (kpos <= qpos_t), 0.0, -jnp.inf)
    kv16 = kv_ref[0].astype(BF16)
    ps = _softmax_rows(_dot_nt(q, kv16) + jnp.concatenate([bias] * H_NSA, axis=0))
    osel = _dot(ps.astype(BF16), kv16)
    start = jnp.clip(i * tq - WINDOW, 0, t - span)
    start = pl.multiple_of(start, 8)
    w16 = win_ref[0, pl.ds(start, span), :].astype(BF16)
    roww = lax.broadcasted_iota(jnp.int32, (rows, span), 0)
    diff = i * tq + (roww & (tq - 1)) - (start + lax.broadcasted_iota(jnp.int32, (rows, span), 1))
    pw = _softmax_rows(jnp.where((diff >= 0) & (diff <= WINDOW), _dot_nt(q, w16), -jnp.inf))
    ow = _dot(pw.astype(BF16), w16)
    g = misc_ref[0]
    heads = []
    for h in range(H_NSA):
        sl = slice(h * tq, (h + 1) * tq)
        c0 = MISC_GATE + 3 * h
        heads.append(g[:, c0:c0 + 1] * oc[sl] + g[:, c0 + 1:c0 + 2] * osel[sl] + g[:, c0 + 2:c0 + 3] * ow[sl])
    pairs = [jnp.where(lane < 64, pltpu.roll(heads[2 * p], 64, 1), heads[2 * p + 1]) for p in range(H_NSA // 2)]
    o_ref[0] = jnp.concatenate(pairs, axis=1).astype(o_ref.dtype)


def nsa_prompt(nq, cmpkv, nsa, win, misc, *, tq=128):
    b, t, _ = nq.shape
    nb = -(-t // L_CMP)
    tq = min(tq, t)
    span = min(t, WINDOW + tq)
    n_sel = min(N_SEL, nb)
    e = (jnp.arange(t)[None, :] // L_CMP == jnp.arange(LANES)[:, None]).astype(BF16)
    return pl.pallas_call(
        functools.partial(_nsa_prompt_body, tq, t, nb, span, n_sel),
        grid=(b, t // tq),
        in_specs=[pl.BlockSpec((1, tq, 256), lambda bb, i: (bb, i, 0)),
                  pl.BlockSpec((1, LANES, LANES), lambda bb, i: (bb, 0, 0)),
                  pl.BlockSpec((1, t, LANES), lambda bb, i: (bb, 0, 1)),
                  pl.BlockSpec((1, t, LANES), lambda bb, i: (bb, 0, 0)),
                  pl.BlockSpec((1, tq, LANES), lambda bb, i: (bb, i, 0)),
                  pl.BlockSpec((LANES, t), lambda bb, i: (0, 0))],
        out_specs=pl.BlockSpec((1, tq, 256), lambda bb, i: (bb, i, 0)),
        out_shape=jax.ShapeDtypeStruct((b, t, 256), BF16),
        compiler_params=_cparams(("arbitrary", "arbitrary")),
        name="nsa_prompt",
    )(nq, cmpkv, nsa, win, misc, e)


def _outproj_body(decode, *refs):
    if decode:
        ol_ref, wuvb_ref, on_ref, of_ref, w_ref, h_ref, m_ref, gpost_ref, o_ref = refs
        om = _dot(ol_ref[...].astype(BF16), wuvb_ref[...]).astype(BF16)
    else:
        om_ref, on_ref, of_ref, w_ref, h_ref, m_ref, gpost_ref, o_ref = refs
        om = om_ref[...]
    g_, r_, d_ = h_ref.shape
    o = jnp.concatenate([om, on_ref[...].astype(BF16), of_ref[...].astype(BF16)], axis=1)
    yn = _rms(_dot(o, w_ref[...]), gpost_ref[...]).reshape(g_, r_, d_)
    o_ref[...] = h_ref[...] + m_ref[2] * yn


def out_proj(o_mla, o_nsa, o_fox, w_out, h, mods, g_post, *, wuv_bd=None, tm_cap=512):
    b, t, d = h.shape
    decode = wuv_bd is not None
    if not decode:
        tm = _tok_tile(t, tm_cap)
        tpb = t // tm
        blk = (1, tm, d)
        hmap = lambda i: (i // tpb, i % tpb, 0)
        mmap = lambda i: (1, i // tpb, 0, 0)
        grid0, mblk = b * tpb, (3, 1, 1, d)
    else:
        gb = min(b, max(1, tm_cap // t))
        tm = gb * t
        blk = (gb, t, d)
        hmap = lambda i: (i, 0, 0)
        mmap = lambda i: (1, i, 0, 0)
        grid0, mblk = b // gb, (3, gb, 1, d)
    row = lambda a: pl.BlockSpec((tm, a.shape[1]), lambda i: (i, 0))
    full = lambda a: pl.BlockSpec(a.shape, lambda i: (0,) * a.ndim)
    ins = [o_mla] + ([wuv_bd] if decode else []) + [o_nsa, o_fox, w_out, h, mods, g_post.reshape(1, d)]
    specs = [row(o_mla)] + ([full(wuv_bd)] if decode else []) + \
            [row(o_nsa), row(o_fox), full(w_out), pl.BlockSpec(blk, hmap), pl.BlockSpec(mblk, mmap),
             pl.BlockSpec((1, d), lambda i: (0, 0))]
    return pl.pallas_call(
        functools.partial(_outproj_body, decode),
        grid=(grid0,),
        in_specs=specs,
        out_specs=pl.BlockSpec(blk, hmap),
        out_shape=jax.ShapeDtypeStruct(h.shape, F32),
        compiler_params=_cparams(("arbitrary",)),
        name="out_proj_dec" if decode else "out_proj",
    )(*ins)


NEW_PAD = LANES


def _page_copy(spec, layer, page, slot, j):
    cache_ref, buf, sem, src_rows, to_lanes, width = spec
    src = cache_ref.at[layer, page]
    if src_rows is not None:
        src = src.at[pl.ds(src_rows[0], src_rows[1])]
    dst = buf.at[slot, :, pl.ds(j * width, width)] if to_lanes else buf.at[slot, pl.ds(j * width, width)]
    return pltpu.make_async_copy(src, dst, sem.at[slot])


def _fetch_pages(pt_ref, bb, slot, layer, n_pages, specs):
    def body(j, c):
        page = pt_ref[bb, j]
        for spec in specs:
            _page_copy(spec, layer, page, slot, j).start()
        return c

    lax.fori_loop(0, n_pages, body, 0)


def _wait_pages(slot, layer, n_pages, specs):
    def body(j, c):
        for spec in specs:
            _page_copy(spec, layer, 0, slot, j).wait()
        return c

    lax.fori_loop(0, n_pages, body, 0)


def _gather_step(pt_ref, layer, n_pages, specs):
    b = pl.program_id(0)
    slot = b % 2

    @pl.when(b == 0)
    def _():
        _fetch_pages(pt_ref, 0, 0, layer, n_pages, specs)

    @pl.when(b + 1 < pl.num_programs(0))
    def _():
        _fetch_pages(pt_ref, b + 1, 1 - slot, layer, n_pages, specs)

    _wait_pages(slot, layer, n_pages, specs)
    return slot


def _pad_new(x):
    dt, w = x.shape
    return jnp.concatenate([x, jnp.zeros((NEW_PAD - dt, w), x.dtype)], axis=0).astype(BF16)


def _new_bias(rows, dt, ok=None):
    r = lax.broadcasted_iota(jnp.int32, (rows, NEW_PAD), 0)
    c = lax.broadcasted_iota(jnp.int32, (rows, NEW_PAD), 1)
    vis = c <= (r % dt)
    if ok is not None:
        vis = vis & ok
    return jnp.where(vis, 0.0, -jnp.inf)


def _joint_softmax_pv(sa, sb, va, vb, va_t=False):
    m = jnp.maximum(jnp.max(sa, axis=-1, keepdims=True), jnp.max(sb, axis=-1, keepdims=True))
    m = jnp.where(m > -jnp.inf, m, 0.0)
    ea, eb = jnp.exp(sa - m), jnp.exp(sb - m)
    d = jnp.sum(ea, axis=-1, keepdims=True) + jnp.sum(eb, axis=-1, keepdims=True)
    oa = _dot_nt(ea.astype(BF16), va) if va_t else _dot(ea.astype(BF16), va)
    o = oa + _dot(eb.astype(BF16), vb)
    return o * (1.0 / jnp.where(d > 0, d, 1.0))


def _mla_dec_body(layer, n_pages, page, pt_ref, qlat_ref, qx_ref, ckvn_ref, misc_ref, ckv_hbm, kr_hbm, o_ref,
                  ckv_buf, kr_buf, sem_c, sem_r):
    dt = qlat_ref.shape[1]
    slot = _gather_step(pt_ref, layer, n_pages, [(ckv_hbm, ckv_buf, sem_c, None, False, page),
                                                 (kr_hbm, kr_buf, sem_r, None, True, page)])
    ql = qlat_ref[0]
    q = jnp.concatenate([ql[:, h * KV_LORA:(h + 1) * KV_LORA] for h in range(H_MLA)], axis=0).astype(BF16)
    lane = lax.broadcasted_iota(jnp.int32, (dt, LANES), 1)
    qxf = qx_ref[0]
    parts = []
    for h in range(H_MLA):
        x = qxf[:, (h // 2) * LANES:(h // 2 + 1) * LANES]
        if h % 2:
            x = pltpu.roll(x, LANES - ROPE_DIM, 1)
        parts.append(jnp.where(lane < ROPE_DIM, x, 0.0))
    qr = jnp.concatenate(parts, axis=0)[:, :ROPE_DIM].astype(BF16)
    ckv16 = ckv_buf[slot].astype(BF16)
    krt16 = kr_buf[slot].astype(BF16)
    ckvn16 = _pad_new(ckvn_ref[0])
    krn16 = _pad_new(misc_ref[0][:, :ROPE_DIM])
    sa = _dot_nt(q, ckv16) + _dot(qr, krt16)
    sb = _dot_nt(q, ckvn16) + _dot_nt(qr, krn16) + _new_bias(H_MLA * dt, dt)
    o = _joint_softmax_pv(sa, sb, ckv16, ckvn16)
    for h in range(H_MLA):
        o_ref[0, :, h * KV_LORA:(h + 1) * KV_LORA] = o[h * dt:(h + 1) * dt].astype(o_ref.dtype)


def mla_decode(layer, page_table, qlat, qx, ckv_new, misc, cache_ckv, cache_krt):
    db, dt, _ = qlat.shape
    n_pages, page = page_table.shape[1], cache_ckv.shape[2]
    past = n_pages * page
    blk = lambda w: pl.BlockSpec((1, dt, w), lambda b, pt: (b, 0, 0))
    any_spec = pl.BlockSpec(memory_space=pl.ANY)
    return pl.pallas_call(
        functools.partial(_mla_dec_body, layer, n_pages, page),
        grid_spec=pltpu.PrefetchScalarGridSpec(
            num_scalar_prefetch=1, grid=(db,),
            in_specs=[blk(H_MLA * KV_LORA), blk(qx.shape[-1]), blk(KV_LORA), blk(LANES), any_spec, any_spec],
            out_specs=blk(H_MLA * KV_LORA),
            scratch_shapes=[pltpu.VMEM((2, past, KV_LORA), F32), pltpu.VMEM((2, ROPE_DIM, past), F32),
                            pltpu.SemaphoreType.DMA((2,)), pltpu.SemaphoreType.DMA((2,))]),
        out_shape=jax.ShapeDtypeStruct((db, dt, H_MLA * KV_LORA), F32),
        compiler_params=_cparams(("arbitrary",)),
        name="mla_decode",
    )(page_table, qlat, qx, ckv_new, misc, cache_ckv, cache_krt)


def _fox_dec_body(layer, n_pages, page, pt_ref, fq_ref, fnew_ref, fcum_ref, kv_hbm, lf_hbm, o_ref,
                  kt_buf, vt_buf, lf_buf, sem_k, sem_v, sem_l):
    dt = fq_ref.shape[1]
    rows = H_FOX * dt
    hw = H_FOX * FOX_DIM
    slot = _gather_step(pt_ref, layer, n_pages, [(kv_hbm, kt_buf, sem_k, (0, hw), True, page),
                                                 (kv_hbm, vt_buf, sem_v, (hw, hw), True, page),
                                                 (lf_hbm, lf_buf, sem_l, None, False, H_FOX)])
    lane = lax.broadcasted_iota(jnp.int32, (dt, hw), 1)
    qf = fq_ref[0]
    q = jnp.concatenate([jnp.where(lane // FOX_DIM == h, qf, 0.0) for h in range(H_FOX)], axis=0).astype(BF16)
    kt16 = kt_buf[slot].astype(BF16)
    vt16 = vt_buf[slot].astype(BF16)
    fnew = fnew_ref[0]
    kn16, vn16 = _pad_new(fnew[:, :hw]), _pad_new(fnew[:, hw:])
    nr = n_pages * H_FOX
    lf = lf_buf[slot]
    ri = lax.broadcasted_iota(jnp.int32, (page, page), 0)
    ci = lax.broadcasted_iota(jnp.int32, (page, page), 1)
    tri = jnp.where(ri > ci, 1.0, 0.0).astype(BF16)
    ones = jnp.ones((page, page), BF16)
    pr = lax.broadcasted_iota(jnp.int32, (nr, nr), 0)
    pc = lax.broadcasted_iota(jnp.int32, (nr, nr), 1)
    upper = jnp.where((pc % H_FOX == pr % H_FOX) & (pc // H_FOX > pr // H_FOX), 1.0, 0.0).astype(BF16)
    l3 = _split3(lf)
    within = _dot(l3[0], tri) + _dot(l3[1], tri) + _dot(l3[2], tri)
    tot = _dot(l3[0], ones) + _dot(l3[1], ones) + _dot(l3[2], ones)
    t3 = _split3(tot)
    r2 = within + _dot(upper, t3[0]) + _dot(upper, t3[1]) + _dot(upper, t3[2])
    bias_rows = []
    for h in range(H_FOX):
        flat = jnp.concatenate([r2[j * H_FOX + h:j * H_FOX + h + 1, :] for j in range(n_pages)], axis=1)
        bias_rows.append(jnp.broadcast_to(flat, (dt, flat.shape[1])))
    fc = fcum_ref[0]
    gt_col = jnp.concatenate([fc[:, MISC_LOGF + h:MISC_LOGF + h + 1] for h in range(H_FOX)], axis=0)
    sa = _dot(q, kt16) + jnp.concatenate(bias_rows, axis=0) + gt_col
    lane_r = lax.broadcasted_iota(jnp.int32, (rows, LANES), 1)
    row_r = lax.broadcasted_iota(jnp.int32, (rows, LANES), 0)
    onehot = jnp.where(lane_r == MISC_LOGF + row_r // dt, 1.0, 0.0).astype(BF16)
    f3 = _split3(jnp.concatenate([fc, jnp.zeros((NEW_PAD - dt, LANES), F32)], axis=0))
    gt_row = _dot_nt(onehot, f3[0]) + _dot_nt(onehot, f3[1]) + _dot_nt(onehot, f3[2])
    sb = _dot_nt(q, kn16) + gt_col - gt_row + _new_bias(rows, dt)
    o = _joint_softmax_pv(sa, sb, vt16, vn16, va_t=True)
    out = jnp.zeros((dt, hw), F32)
    for h in range(H_FOX):
        out = out + jnp.where(lane // FOX_DIM == h, o[h * dt:(h + 1) * dt], 0.0)
    o_ref[0] = out.astype(o_ref.dtype)


def fox_decode(layer, page_table, fq, fox_new, fcum, cache_kvt, cache_lft):
    db, dt, hw = fq.shape
    n_pages, page = page_table.shape[1], cache_kvt.shape[3]
    past = n_pages * page
    blk = lambda w: pl.BlockSpec((1, dt, w), lambda b, pt: (b, 0, 0))
    any_spec = pl.BlockSpec(memory_space=pl.ANY)
    return pl.pallas_call(
        functools.partial(_fox_dec_body, layer, n_pages, page),
        grid_spec=pltpu.PrefetchScalarGridSpec(
            num_scalar_prefetch=1, grid=(db,),
            in_specs=[blk(hw), blk(2 * hw), blk(LANES), any_spec, any_spec],
            out_specs=blk(hw),
            scratch_shapes=[pltpu.VMEM((2, hw, past), F32), pltpu.VMEM((2, hw, past), F32),
                            pltpu.VMEM((2, n_pages * H_FOX, page), F32),
                            pltpu.SemaphoreType.DMA((2,)), pltpu.SemaphoreType.DMA((2,)),
                            pltpu.SemaphoreType.DMA((2,))]),
        out_shape=jax.ShapeDtypeStruct((db, dt, hw), F32),
        compiler_params=_cparams(("arbitrary",)),
        name="fox_decode",
    )(page_table, fq, fox_new, fcum, cache_kvt, cache_lft)


def _cmp_weight_pairs(w_cmp):
    z = jnp.zeros((L_CMP, NSA_DIM, NSA_DIM), w_cmp.dtype)
    top = jnp.concatenate([w_cmp[0], z], axis=2)
    bot = jnp.concatenate([z, w_cmp[1]], axis=2)
    return jnp.concatenate([top, bot], axis=1).reshape(L_CMP * LANES, LANES).astype(BF16)


def _cmp_weight_paged(w, bpp):
    eye = jnp.eye(bpp, dtype=w.dtype)
    v = jnp.einsum("lde,pq->dplqe", w, eye)
    return v.reshape(NSA_DIM * bpp * L_CMP, bpp * NSA_DIM).astype(BF16)


def _nsa_dec_body(layer, n_pages, page, n_sel, lane_bids, pt_ref, nq_ref, new_ref, wnew_ref, misc_ref, wpast_ref,
                  wl_ref, vk_ref, vv_ref, e_ref, bid_ref, nsa_hbm, o_ref, cmp_buf, sel_buf, xk_scr, xv_scr,
                  sem_c, sem_s):
    dt = nq_ref.shape[1]
    past = n_pages * page
    nbp = past // L_CMP
    nb = nbp + 1
    nbl = e_ref.shape[0]
    rows = H_NSA * dt
    d2 = 2 * NSA_DIM
    slot = _gather_step(pt_ref, layer, n_pages, [(nsa_hbm, cmp_buf, sem_c, (0, d2), False, d2),
                                                 (nsa_hbm, sel_buf, sem_s, (d2, d2), True, page)])
    lane = lax.broadcasted_iota(jnp.int32, (dt, LANES), 1)
    q = _head_slabs(nq_ref[0], lane)[:, :NSA_DIM].astype(BF16)
    new = new_ref[0]
    for d in range(NSA_DIM):
        xk_scr[:, d * page:(d + 1) * page] = cmp_buf[slot, pl.ds(d, n_pages, stride=d2), :].astype(BF16)
        xv_scr[:, d * page:(d + 1) * page] = cmp_buf[slot, pl.ds(NSA_DIM + d, n_pages, stride=d2), :].astype(BF16)
    sk = _dot(xk_scr[...], vk_ref[...])
    sv = _dot(xv_scr[...], vv_ref[...])
    part = jnp.zeros((1, LANES), F32)
    for l in range(dt):
        part = part + _dot(jnp.broadcast_to(new[l:l + 1, :LANES], (8, LANES)).astype(BF16),
                           wl_ref[l * LANES:(l + 1) * LANES, :])[0:1]
    tail_row = lax.broadcasted_iota(jnp.int32, (nbl - nbp, LANES), 0)
    tail = jnp.where(tail_row == 0, jnp.broadcast_to(part, (nbl - nbp, LANES)), 0.0)
    kc16 = jnp.concatenate([sk[:, :NSA_DIM], sk[:, NSA_DIM:], tail[:, :NSA_DIM]], axis=0).astype(BF16)
    vc16 = jnp.concatenate([sv[:, :NSA_DIM], sv[:, NSA_DIM:], tail[:, NSA_DIM:]], axis=0).astype(BF16)
    bid4 = jnp.broadcast_to(bid_ref[...], (rows, nbl))
    qpos4 = past + lax.broadcasted_iota(jnp.int32, (rows, nbl), 0) % dt
    done = ((bid4 + 1) * L_CMP - 1 <= qpos4) & (bid4 < nb)
    pc = _softmax_rows(jnp.where(done, _dot_nt(q, kc16), -jnp.inf))
    oc = _dot(pc.astype(BF16), vc16)
    imp = pc[0:dt]
    for h in range(1, H_NSA):
        imp = imp + pc[h * dt:(h + 1) * dt]
    bid = jnp.broadcast_to(bid_ref[...], (dt, nbl))
    cur = (past + lax.broadcasted_iota(jnp.int32, (dt, nbl), 0)) // L_CMP
    forced = (bid == 0) | (bid == cur) | (bid == cur - 1)
    valid = (bid <= cur) & (bid < nb)
    imp = jnp.where(valid, jnp.where(forced, FORCED_SCORE, imp), -jnp.inf)
    sel = _select_blocks(imp, valid, bid, lane_bids, n_sel)
    seltok = _dot(sel.astype(BF16), e_ref[...])
    bias = jnp.where(seltok > 0.5, 0.0, -jnp.inf)
    kvt16 = sel_buf[slot].astype(BF16)
    kn16 = _pad_new(new[:, 2 * NSA_DIM:3 * NSA_DIM])
    vn16 = _pad_new(new[:, 3 * NSA_DIM:])
    new_lane = 2 * (nbp // 2) if nbp % 2 == 0 else None
    sel_new = jnp.concatenate([sel[:, new_lane:new_lane + 1]] * H_NSA, axis=0) > 0.5
    sa = _dot(q, kvt16[:NSA_DIM]) + jnp.concatenate([bias] * H_NSA, axis=0)
    sb = _dot_nt(q, kn16) + _new_bias(rows, dt, sel_new)
    osel = _joint_softmax_pv(sa, sb, kvt16[NSA_DIM:], vn16, va_t=True)
    wpt16 = wpast_ref[0, 0].astype(BF16)
    wn = wnew_ref[0]
    wb = wpt16.shape[1]
    qpw = past + lax.broadcasted_iota(jnp.int32, (rows, wb), 0) % dt
    kpw = past - wb + lax.broadcasted_iota(jnp.int32, (rows, wb), 1)
    okw = (qpw - kpw <= WINDOW) & (kpw >= 0)
    sa = jnp.where(okw, _dot(q, wpt16[:NSA_DIM]), -jnp.inf)
    sb = _dot_nt(q, _pad_new(wn[:, :NSA_DIM])) + _new_bias(rows, dt)
    ow = _joint_softmax_pv(sa, sb, wpt16[NSA_DIM:], _pad_new(wn[:, NSA_DIM:]), va_t=True)
    g = misc_ref[0]
    heads = []
    for h in range(H_NSA):
        sl = slice(h * dt, (h + 1) * dt)
        c0 = MISC_GATE + 3 * h
        heads.append(g[:, c0:c0 + 1] * oc[sl] + g[:, c0 + 1:c0 + 2] * osel[sl] + g[:, c0 + 2:c0 + 3] * ow[sl])
    o_ref[0] = jnp.concatenate(heads, axis=1).astype(o_ref.dtype)


def nsa_decode(layer, page_table, nq, nsa_new, win_new, misc, win_t, wl, vk, vv, cache_nsat):
    db, dt, _ = nq.shape
    n_pages, page = page_table.shape[1], cache_nsat.shape[3]
    past = n_pages * page
    assert page == 2 * L_CMP and dt <= min(L_CMP, WINDOW, NEW_PAD)
    wb = win_t.shape[3]
    nbp = past // L_CMP
    nb = nbp + 1
    nbl = -(-nb // LANES) * LANES
    n_sel = min(N_SEL, nb)
    bids = np.full((nbl,), nbl + nb, np.int32)
    bids[:n_pages] = 2 * np.arange(n_pages)
    bids[n_pages:2 * n_pages] = 2 * np.arange(n_pages) + 1
    bids[2 * n_pages] = nbp
    lane_bids = tuple((int(i), int(bids[i])) for i in range(2 * n_pages + 1))
    e = (jnp.arange(past)[None, :] // L_CMP == jnp.asarray(bids)[:, None]).astype(BF16)
    blk = lambda w: pl.BlockSpec((1, dt, w), lambda b, pt: (b, 0, 0))
    full2 = lambda a: pl.BlockSpec(a.shape, lambda b, pt: (0, 0))
    bid_arr = jnp.asarray(bids).reshape(1, nbl)
    return pl.pallas_call(
        functools.partial(_nsa_dec_body, layer, n_pages, page, n_sel, lane_bids),
        grid_spec=pltpu.PrefetchScalarGridSpec(
            num_scalar_prefetch=1, grid=(db,),
            in_specs=[blk(256), blk(256), blk(LANES), blk(LANES),
                      pl.BlockSpec((1, 1, 2 * NSA_DIM, wb), lambda b, pt: (layer, b, 0, 0)),
                      full2(wl), full2(vk), full2(vv), full2(e), full2(bid_arr),
                      pl.BlockSpec(memory_space=pl.ANY)],
            out_specs=blk(256),
            scratch_shapes=[pltpu.VMEM((2, n_pages * 2 * NSA_DIM, page), F32), pltpu.VMEM((2, 2 * NSA_DIM, past), F32),
                            pltpu.VMEM((n_pages, NSA_DIM * page), BF16), pltpu.VMEM((n_pages, NSA_DIM * page), BF16),
                            pltpu.SemaphoreType.DMA((2,)), pltpu.SemaphoreType.DMA((2,))]),
        out_shape=jax.ShapeDtypeStruct((db, dt, 256), F32),
        compiler_params=_cparams(("arbitrary",)),
        name="nsa_decode",
    )(page_table, nq, nsa_new, win_new, misc, win_t, wl, vk, vv, e, bid_arr, cache_nsat)


def _mixer_prompt(h, mods, g_pre, g_post, lw, consts):
    b, t, d = h.shape
    z = in_proj(h, mods, g_pre, lw, consts["tab_p"], lw["bvec"], consts["ltri_p"], consts["place"], decode=False)
    r3 = lambda a: a.reshape(b, t, a.shape[-1])
    o_mla = flash_pairs(r3(z["qn"]), r3(z["qx"]), r3(z["kn"]), r3(z["kx"]), r3(z["vm"]),
                        xw=ROPE_DIM, kx_shared=True)
    fkv16 = r3(z["fkv16"])
    o_fox = flash_pairs(r3(z["fq"]), r3(z["fqx"]), fkv16, r3(z["fkx"]), fkv16,
                        xw=FOX_XW, kx_shared=False, v_off=H_FOX // 2)
    nb = t // L_CMP
    cmpkv = compress_blocks(z["nsa"].reshape(b * nb, L_CMP * 256), lw["wbig"]).reshape(b, nb, LANES)
    cmpkv = jnp.pad(cmpkv, ((0, 0), (0, LANES - nb), (0, 0)))
    o_nsa = nsa_prompt(r3(z["nq"]), cmpkv, r3(z["nsa"]), r3(z["win"]), r3(z["misc"]))
    n = b * t
    h_new = out_proj(o_mla.reshape(n, -1), o_nsa.reshape(n, -1), o_fox.reshape(n, -1), lw["w_out"], h, mods, g_post)
    wb = min(WINDOW, t)
    misc = r3(z["misc"])
    state = (r3(z["ckv"]), misc[:, :, :ROPE_DIM], r3(z["nsa"]).reshape(b, t, 4, NSA_DIM),
             r3(z["win"])[:, t - wb:].reshape(b, wb, 2, NSA_DIM),
             r3(z["fox"]).reshape(b, t, 2, H_FOX, FOX_DIM), misc[:, :, MISC_LOGF:MISC_END])
    return h_new, state


def _mixer_sample(h, mods, g_pre, g_post, lw, consts, l, page_table, c_ckv, c_kr, c_nsa, s_win, c_fkv, c_flf):
    db, dt, d = h.shape
    z = in_proj(h, mods, g_pre, lw, consts["tab_s"], lw["bvec"], consts["ltri_s"], consts["place"], decode=True)
    r3 = lambda a: a.reshape(db, dt, a.shape[-1])
    misc = r3(z["misc"])
    o_lat = mla_decode(l, page_table, r3(z["qlat"]), r3(z["qx"]), r3(z["ckv"]), misc, c_ckv, c_kr)
    o_nsa = nsa_decode(l, page_table, r3(z["nq"]), r3(z["nsa"]), r3(z["win"]), misc, consts["win_t"], lw["wl"],
                       lw["vk"], lw["vv"], c_nsa)
    o_fox = fox_decode(l, page_table, r3(z["fq"]), r3(z["fox"]), r3(z["fcum"]), c_fkv, c_flf)
    win_new = jnp.concatenate([s_win[l, :, dt:], r3(z["win"]).reshape(db, dt, 2, NSA_DIM)], axis=1)
    n = db * dt
    h_new = out_proj(o_lat.reshape(n, -1), o_nsa.reshape(n, -1), o_fox.reshape(n, -1), lw["w_out"], h, mods, g_post,
                     wuv_bd=lw["wuv_bd"])
    state = (r3(z["ckv"]), misc[:, :, :ROPE_DIM], r3(z["nsa"]).reshape(db, dt, 4, NSA_DIM),
             win_new, r3(z["fox"]).reshape(db, dt, 2, H_FOX, FOX_DIM),
             misc[:, :, MISC_LOGF:MISC_END])
    return h_new, state


def _layer_weights(l, w_in, b_fox_f, mla_g_q, mla_g_kv, mla_w_uq, mla_w_uk, mla_w_uv, nsa_w_cmp, w_out):
    lw = _inproj_weights(w_in[l], mla_w_uq[l], mla_w_uk[l], mla_w_uv[l])
    bvec = jnp.zeros((1, LANES), F32).at[0, MISC_LOGF:MISC_END].set(b_fox_f[l])
    eye = jnp.eye(H_MLA, dtype=F32)
    wuv_bd = jnp.einsum("chd,hg->hcgd", mla_w_uv[l], eye).reshape(H_MLA * KV_LORA, H_MLA * V_DIM)
    lw.update(g_q=mla_g_q[l], g_kv=mla_g_kv[l], bvec=bvec, wbig=_cmp_weight(nsa_w_cmp[l]),
              w_out=w_out[l].astype(BF16), wuv_bd=wuv_bd.astype(BF16), wl=_cmp_weight_pairs(nsa_w_cmp[l]),
              vk=_cmp_weight_paged(nsa_w_cmp[l, 0], 2), vv=_cmp_weight_paged(nsa_w_cmp[l, 1], 2))
    return lw


def kernel(x_prompt, x_sample, cache_mla_ckv, cache_mla_krope, cache_nsa_kv, state_nsa_win, cache_fox_kv,
           cache_fox_logf, page_table, c_prompt, c_sample, ada_w, ada_b, norm_pre, norm_post, ffn_w_gate,
           ffn_w_up, ffn_w_down, w_in, b_fox_f, mla_g_q, mla_g_kv, mla_w_uq, mla_w_uk, mla_w_uv, nsa_w_cmp, w_out):
    depth = w_in.shape[0]
    b, t, d = x_prompt.shape
    db, dt, _ = x_sample.shape
    past = page_table.shape[1] * cache_mla_ckv.shape[2]
    mods_all = ada_mods(jnp.concatenate([c_prompt, c_sample], axis=0), ada_w, ada_b)
    wg16, wu16, wd16 = ffn_w_gate.astype(BF16), ffn_w_up.astype(BF16), ffn_w_down.astype(BF16)
    tm_p = min(t, 512)
    gb = min(db, max(1, 512 // dt))
    tm_s = gb * dt
    ii = jnp.arange(tm_s)
    consts = dict(
        tab_p=_rope_tables(jnp.arange(t, dtype=jnp.int32)),
        tab_s=jnp.tile(_rope_tables(past + jnp.arange(dt, dtype=jnp.int32)), (gb, 1)),
        ltri_p=(jnp.arange(tm_p)[:, None] >= jnp.arange(tm_p)[None, :]).astype(BF16),
        ltri_s=((ii[:, None] >= ii[None, :]) & (ii[:, None] // dt == ii[None, :] // dt)).astype(BF16),
        place=_fox_place(),
        win_t=jnp.transpose(state_nsa_win, (0, 1, 3, 4, 2)).reshape(depth, db, 2 * NSA_DIM, -1))
    pool, page = cache_mla_ckv.shape[1:3]
    cache_krt = jnp.transpose(cache_mla_krope, (0, 1, 3, 2))
    cache_nsat = jnp.transpose(cache_nsa_kv, (0, 1, 3, 4, 2)).reshape(depth, pool, 4 * NSA_DIM, page)
    cache_fkvt = jnp.transpose(cache_fox_kv, (0, 1, 3, 4, 5, 2)).reshape(depth, pool, 2 * H_FOX * FOX_DIM, page)
    cache_lft = jnp.transpose(cache_fox_logf, (0, 1, 3, 2))
    hp, hs = x_prompt, x_sample
    st_p, st_s = [], []
    for l in range(depth):
        lw = _layer_weights(l, w_in, b_fox_f, mla_g_q, mla_g_kv, mla_w_uq, mla_w_uk, mla_w_uv, nsa_w_cmp, w_out)
        mp = mods_all[l, :, :b].reshape(3 * N_SUB, b, 1, d)
        ms = mods_all[l, :, b:].reshape(3 * N_SUB, db, 1, d)
        hp = ffn_half(hp, mp, 0, norm_pre[l, 0], norm_post[l, 0], wg16, wu16, wd16, l, 0)
        hs = ffn_half(hs, ms, 0, norm_pre[l, 0], norm_post[l, 0], wg16, wu16, wd16, l, 0)
        hp, sp = _mixer_prompt(hp, mp, norm_pre[l, 1], norm_post[l, 1], lw, consts)
        hs, ss = _mixer_sample(hs, ms, norm_pre[l, 1], norm_post[l, 1], lw, consts, l, page_table,
                               cache_mla_ckv, cache_krt, cache_nsat, state_nsa_win, cache_fkvt, cache_lft)
        hp = ffn_half(hp, mp, 2, norm_pre[l, 2], norm_post[l, 2], wg16, wu16, wd16, l, 1)
        hs = ffn_half(hs, ms, 2, norm_pre[l, 2], norm_post[l, 2], wg16, wu16, wd16, l, 1)
        st_p.append(sp)
        st_s.append(ss)
    outs_p = tuple(jnp.stack(a) for a in zip(*st_p))
    outs_s = tuple(jnp.stack(a) for a in zip(*st_s))
    return (hp, hs) + outs_p + outs_s
```

```python
import functools

import numpy as np
import jax
import jax.numpy as jnp
from jax import lax
from jax.experimental import pallas as pl
from jax.experimental.pallas import tpu as pltpu

F32, BF16 = jnp.float32, jnp.bfloat16

H_MLA, Q_LORA, KV_LORA, NOPE_DIM, ROPE_DIM, V_DIM = 8, 384, 256, 64, 32, 64
H_NSA, NSA_DIM, L_CMP, N_SEL, WINDOW = 4, 64, 64, 16, 512
H_FOX, FOX_DIM = 4, 64
N_SUB = 3
ROPE_THETA = 10000.0
FORCED_SCORE = 1.0e4
EPS = 1e-6
MLA_SCALE = (NOPE_DIM + ROPE_DIM) ** -0.5
NSA_SCALE = NSA_DIM ** -0.5
FOX_SCALE = FOX_DIM ** -0.5
IN_SPLITS = (Q_LORA, KV_LORA, ROPE_DIM, H_NSA * NSA_DIM, 6 * NSA_DIM, H_NSA * 3,
             H_FOX * FOX_DIM, H_FOX * FOX_DIM, H_FOX * FOX_DIM, H_FOX)

LANES = 128
VMEM_LIMIT = 56 * 1024 * 1024

MISC_GATE = ROPE_DIM
MISC_LOGF = ROPE_DIM + H_NSA * 3
MISC_END = MISC_LOGF + H_FOX
FOX_XW = 6

_W_WIDTHS = (("cq", Q_LORA), ("ckv", KV_LORA), ("nq", 256), ("nqs", 256), ("nkv", 384), ("nkvs", 384),
             ("fq", 256), ("fkv", 512), ("misc", 128), ("miscs", 128), ("krp", 128), ("krps", 128))
_W_OFF = {}
_acc = 0
for _n, _w in _W_WIDTHS:
    _W_OFF[_n] = (_acc, _acc + _w)
    _acc += _w
W_TOTAL = _acc
_TAB = {n: (i * LANES, (i + 1) * LANES) for i, n in
        enumerate(("ckx", "skx", "cm", "sm", "cn", "sn", "cnkv", "snkv"))}
TAB_TOTAL = 8 * LANES


def _cparams(sem):
    return pltpu.CompilerParams(dimension_semantics=sem, vmem_limit_bytes=VMEM_LIMIT)


def _rms(x, g):
    ms = jnp.mean(x * x, axis=-1, keepdims=True)
    return x * lax.rsqrt(ms + EPS) * g


def _silu(x):
    return x / (1.0 + jnp.exp(-x))


def _softmax_rows(s):
    m = jnp.max(s, axis=-1, keepdims=True)
    m = jnp.where(m > -jnp.inf, m, 0.0)
    e = jnp.exp(s - m)
    d = jnp.sum(e, axis=-1, keepdims=True)
    return e * (1.0 / jnp.where(d > 0, d, 1.0))


def _dot(a, b):
    return jnp.dot(a, b, preferred_element_type=F32)


def _dot_nt(a, b):
    return lax.dot_general(a, b, (((1,), (1,)), ((), ())), preferred_element_type=F32)


def _split3(x):
    hi = x.astype(BF16)
    r1 = x - hi.astype(F32)
    mid = r1.astype(BF16)
    lo = (r1 - mid.astype(F32)).astype(BF16)
    return hi, mid, lo


def _ada_body(c_ref, w_ref, b_ref, o_ref):
    s = _silu(c_ref[...])
    o_ref[0, 0] = _dot(s.astype(BF16), w_ref[0].astype(BF16)) + b_ref[0, 0]


def ada_mods(c_all, ada_w, ada_b):
    depth, d, n9 = ada_w.shape
    nc = n9 // d
    bt = c_all.shape[0]
    return pl.pallas_call(
        _ada_body,
        grid=(depth, nc),
        in_specs=[pl.BlockSpec((bt, d), lambda l, n: (0, 0)),
                  pl.BlockSpec((1, d, d), lambda l, n: (l, 0, n)),
                  pl.BlockSpec((1, 1, 1, d), lambda l, n: (l, n, 0, 0))],
        out_specs=pl.BlockSpec((1, 1, bt, d), lambda l, n: (l, n, 0, 0)),
        out_shape=jax.ShapeDtypeStruct((depth, nc, bt, d), F32),
        compiler_params=_cparams(("arbitrary", "arbitrary")),
        name="ada_mods",
    )(c_all, ada_w, ada_b.reshape(depth, nc, 1, d))


def _ffn_body(h_ref, m_ref, gpre_ref, gpost_ref, wg_ref, wu_ref, wd_ref, o_ref, u_scr, acc_scr):
    f = pl.program_id(1)
    g_, r_, d_ = h_ref.shape

    @pl.when(f == 0)
    def _():
        y = _rms(h_ref[...], gpre_ref[...])
        u = y * (1.0 + m_ref[1]) + m_ref[0]
        u_scr[...] = u.reshape(g_ * r_, d_).astype(BF16)
        acc_scr[...] = jnp.zeros_like(acc_scr)

    u = u_scr[...]
    gate = _dot(u, wg_ref[0, 0])
    up = _dot(u, wu_ref[0, 0])
    a = (_silu(gate) * up).astype(BF16)
    acc_scr[...] += _dot(a, wd_ref[0, 0])

    @pl.when(f == pl.num_programs(1) - 1)
    def _():
        yn = _rms(acc_scr[...], gpost_ref[...]).reshape(g_, r_, d_)
        o_ref[...] = h_ref[...] + 0.5 * m_ref[2] * yn


def _tok_tile(n_rows_per_group, cap):
    return min(n_rows_per_group, cap)


def ffn_half(h, mods, sub, g_pre, g_post, wg, wu, wd, layer, half, *, tm_cap=1024, tf=256):
    b, t, d = h.shape
    dff = wg.shape[-1]
    tf = min(tf, dff)
    if t >= 8 * 16:
        tm = _tok_tile(t, tm_cap)
        blk, tpb = (1, tm, d), t // tm
        hmap = lambda i, f: (i // tpb, i % tpb, 0)
        mmap = lambda i, f: (sub, i // tpb, 0, 0)
        grid0, mblk = b * tpb, (3, 1, 1, d)
    else:
        gb = min(b, max(1, tm_cap // t))
        blk = (gb, t, d)
        hmap = lambda i, f: (i, 0, 0)
        mmap = lambda i, f: (sub, i, 0, 0)
        grid0, mblk = b // gb, (3, gb, 1, d)
    rows = blk[0] * blk[1]
    return pl.pallas_call(
        _ffn_body,
        grid=(grid0, dff // tf),
        in_specs=[pl.BlockSpec(blk, hmap),
                  pl.BlockSpec(mblk, mmap),
                  pl.BlockSpec((1, d), lambda i, f: (0, 0)),
                  pl.BlockSpec((1, d), lambda i, f: (0, 0)),
                  pl.BlockSpec((1, 1, d, tf), lambda i, f: (layer, half, 0, f)),
                  pl.BlockSpec((1, 1, d, tf), lambda i, f: (layer, half, 0, f)),
                  pl.BlockSpec((1, 1, tf, d), lambda i, f: (layer, half, f, 0))],
        out_specs=pl.BlockSpec(blk, hmap),
        out_shape=jax.ShapeDtypeStruct(h.shape, F32),
        scratch_shapes=[pltpu.VMEM((rows, d), BF16), pltpu.VMEM((rows, d), F32)],
        compiler_params=_cparams(("arbitrary", "arbitrary")),
        name="ffn_half",
    )(h, mods, g_pre.reshape(1, d), g_post.reshape(1, d), wg, wu, wd)


def _swap_halves(w, dh):
    k = w.shape[0]
    w4 = w.reshape(k, w.shape[1] // dh, 2, dh // 2)
    return w4[:, :, ::-1, :].reshape(k, -1)


def _inproj_weights(w_in, w_uq, w_uk, w_uv):
    k = w_in.shape[0]
    cuts = np.cumsum(IN_SPLITS)[:-1].tolist()
    cq, ckv, kr, nq, nkv, ng, fq, fk, fv, ff = jnp.split(w_in, cuts, axis=1)
    z = lambda n: jnp.zeros((k, n), w_in.dtype)
    nkv3 = nkv.reshape(k, 3, 2, NSA_DIM)
    nkvs = jnp.stack([_swap_halves(nkv3[:, :, 0].reshape(k, -1), NSA_DIM).reshape(k, 3, NSA_DIM),
                      jnp.zeros((k, 3, NSA_DIM), w_in.dtype)], axis=2).reshape(k, -1)
    krs = _swap_halves(kr, ROPE_DIM)
    cols = dict(cq=cq, ckv=ckv, nq=nq, nqs=_swap_halves(nq, NSA_DIM), nkv=nkv, nkvs=nkvs, fq=fq,
                fkv=jnp.concatenate([fk, fv], axis=1),
                misc=jnp.concatenate([kr, ng, ff, z(LANES - MISC_END)], axis=1),
                miscs=jnp.concatenate([krs, z(LANES - ROPE_DIM)], axis=1),
                krp=jnp.concatenate([kr, kr, z(LANES - 2 * ROPE_DIM)], axis=1),
                krps=jnp.concatenate([krs, krs, z(LANES - 2 * ROPE_DIM)], axis=1))
    w_all = jnp.concatenate([cols[n] for n, _ in _W_WIDTHS], axis=1).astype(BF16)
    dq = NOPE_DIM + ROPE_DIM
    uq = w_uq.reshape(Q_LORA, H_MLA, dq)
    wqn = uq[:, :, :NOPE_DIM].reshape(Q_LORA, -1)
    rope = uq[:, :, NOPE_DIM:]
    zr = jnp.zeros((Q_LORA, H_MLA // 2, LANES - 2 * ROPE_DIM), w_uq.dtype)

    def pairs(r):
        return jnp.concatenate([r.reshape(Q_LORA, H_MLA // 2, 2 * ROPE_DIM), zr], axis=2).reshape(Q_LORA, -1)

    wqr = pairs(rope)
    wqs = pairs(_swap_halves(rope.reshape(Q_LORA, -1), ROPE_DIM).reshape(Q_LORA, H_MLA, ROPE_DIM))
    wuk = w_uk.reshape(KV_LORA, -1)
    wuv = w_uv.reshape(KV_LORA, -1)
    ukt = jnp.transpose(w_uk, (1, 2, 0))
    zt = jnp.zeros_like(ukt)
    even = jnp.concatenate([ukt, zt], axis=1)
    odd = jnp.concatenate([zt, ukt], axis=1)
    wukt = jnp.where((jnp.arange(H_MLA) % 2 == 0)[:, None, None], even, odd)
    return dict(w_all=w_all, wqn=wqn.astype(BF16), wqr=wqr.astype(BF16), wqs=wqs.astype(BF16),
                wuk=wuk.astype(BF16), wuv=wuv.astype(BF16), wukt=wukt.astype(BF16))


def _rope_tables(pos):
    def cs(dh):
        half = dh // 2
        inv = jnp.power(ROPE_THETA, -jnp.arange(half, dtype=F32) / half)
        ang = pos.astype(F32)[:, None] * inv[None, :]
        c, s = jnp.cos(ang), jnp.sin(ang)
        return jnp.concatenate([c, c], 1), jnp.concatenate([-s, s], 1)

    n = pos.shape[0]
    c32, s32 = cs(ROPE_DIM)
    c64, s64 = cs(NSA_DIM)
    one, zero = jnp.ones((n, 1), F32), jnp.zeros((n, 1), F32)
    rep = lambda a, k: jnp.tile(a, (1, k))
    tabs = dict(
        ckx=jnp.concatenate([c32, c32, rep(zero, 64)], 1), skx=jnp.concatenate([s32, s32, rep(zero, 64)], 1),
        cm=jnp.concatenate([c32, rep(one, 96)], 1), sm=jnp.concatenate([s32, rep(zero, 96)], 1),
        cn=NSA_SCALE * jnp.concatenate([c64, c64], 1), sn=NSA_SCALE * jnp.concatenate([s64, s64], 1),
        cnkv=jnp.concatenate([c64, rep(one, 64)], 1), snkv=jnp.concatenate([s64, rep(zero, 64)], 1))
    return jnp.concatenate([tabs[k] for k in _TAB], axis=1)


def _fox_place():
    pq = np.zeros((3 * LANES, 2 * LANES), np.float32)
    pk = np.zeros((3 * LANES, 2 * LANES), np.float32)
    cq = np.zeros((1, 2 * LANES), np.float32)
    ck = np.zeros((1, 2 * LANES), np.float32)
    for h in range(H_FOX):
        base = (h // 2) * LANES + (h % 2) * FOX_XW
        for s in range(3):
            pq[s * LANES + MISC_LOGF + h, base + s] = 1.0
            pk[s * LANES + MISC_LOGF + h, base + 3 + s] = -1.0
            cq[0, base + 3 + s] = 1.0
            ck[0, base + s] = 1.0
    return (jnp.asarray(pq, BF16), jnp.asarray(pk, BF16), jnp.asarray(cq), jnp.asarray(ck))


def _inproj_body(decode, tpb, *refs):
    (h_ref, m_ref, gpre_ref, w_ref, gq_ref, gkv_ref, wqn_ref, wqr_ref, wqs_ref, wuk_ref, wuv_ref, wukt_ref,
     tab_ref, bvec_ref, ltri_ref, pq_ref, pk_ref, cq_ref, ck_ref) = refs[:19]
    outs = refs[19:-1]
    carry_scr = refs[-1]
    (ckv_o, nsa_o, win_o, fox_o, misc_o, qn_o, qx_o, kx_o, nq_o, fq_o, fkv16_o) = outs[:11]
    g_, r_, d_ = h_ref.shape
    tm = g_ * r_
    y = _rms(h_ref[...], gpre_ref[...])
    u = (y * (1.0 + m_ref[1]) + m_ref[0]).reshape(tm, d_).astype(BF16)

    def z(name):
        a, b = _W_OFF[name]
        return _dot(u, w_ref[:, a:b])

    def tab(name, k=1):
        a, b = _TAB[name]
        t = tab_ref[:, a:b]
        return t if k == 1 else jnp.concatenate([t] * k, axis=1)

    cqn = _rms(z("cq"), gq_ref[...]).astype(BF16)
    qn = _dot(cqn, wqn_ref[...]) * MLA_SCALE
    qn16 = qn.astype(BF16)
    qn_o[...] = qn16.astype(qn_o.dtype)
    qx = (_dot(cqn, wqr_ref[...]) * tab("ckx", 4) + _dot(cqn, wqs_ref[...]) * tab("skx", 4)) * MLA_SCALE
    qx_o[...] = qx.astype(qx_o.dtype)
    ckvn = _rms(z("ckv"), gkv_ref[...])
    ckv_o[...] = ckvn
    kx_o[...] = (z("krp") * tab("ckx") + z("krps") * tab("skx")).astype(kx_o.dtype)
    if decode:
        qlat_o, fcum_o = outs[11:]
        for h in range(H_MLA):
            p = h // 2
            qlat_o[:, h * KV_LORA:(h + 1) * KV_LORA] = _dot(qn16[:, p * LANES:(p + 1) * LANES], wukt_ref[h])
    else:
        kn_o, vm_o, fqx_o, fkx_o = outs[11:]
        ck16 = ckvn.astype(BF16)
        kn_o[...] = _dot(ck16, wuk_ref[...]).astype(BF16)
        vm_o[...] = _dot(ck16, wuv_ref[...]).astype(BF16)
    nq_o[...] = (z("nq") * tab("cn", 2) + z("nqs") * tab("sn", 2)).astype(nq_o.dtype)
    nkv = z("nkv") * tab("cnkv", 3) + z("nkvs") * tab("snkv", 3)
    nsa_o[...] = nkv[:, :4 * NSA_DIM]
    win_o[...] = nkv[:, 4 * NSA_DIM:]
    fq_o[...] = (z("fq") * FOX_SCALE).astype(fq_o.dtype)
    fkv = z("fkv")
    fox_o[...] = fkv
    fkv16_o[...] = fkv.astype(fkv16_o.dtype)
    zm = z("misc")
    lane = lax.broadcasted_iota(jnp.int32, (tm, LANES), 1)
    roped = zm * tab("cm") + z("miscs") * tab("sm")
    xb = zm + bvec_ref[...]
    logsig = jnp.minimum(xb, 0.0) - jnp.log(1.0 + jnp.exp(-jnp.abs(xb)))
    misc = jnp.where(lane < MISC_GATE, roped,
                     jnp.where(lane < MISC_LOGF, 1.0 / (1.0 + jnp.exp(-zm)),
                               jnp.where(lane < MISC_END, logsig, 0.0)))
    misc_o[...] = misc
    lf = jnp.where((lane >= MISC_LOGF) & (lane < MISC_END), misc, 0.0)
    ltri = ltri_ref[...]
    hi, mid, lo = _split3(lf)
    cum = _dot(ltri, hi) + _dot(ltri, mid) + _dot(ltri, lo)
    first = (pl.program_id(0) % tpb) == 0
    carry = jnp.where(first, 0.0, carry_scr[0:1, :])
    fc = cum + carry
    carry_scr[0:1, :] = fc[tm - 1:tm, :]
    if decode:
        fcum_o[...] = fc
    else:
        g3 = jnp.concatenate(list(_split3(fc)), axis=1)
        fqx_o[...] = (_dot(g3, pq_ref[...]) + cq_ref[...]).astype(BF16)
        fkx_o[...] = (_dot(g3, pk_ref[...]) + ck_ref[...]).astype(BF16)


def in_proj(h, mods, g_pre, lw, tables, bvec, ltri, place, *, decode, tm_cap=512):
    b, t, d = h.shape
    n = b * t
    if not decode:
        tm = _tok_tile(t, tm_cap)
        tpb = t // tm
        blk = (1, tm, d)
        hmap = lambda i: (i // tpb, i % tpb, 0)
        mmap = lambda i: (1, i // tpb, 0, 0)
        tmap = lambda i: (i % tpb, 0)
        grid0, mblk = b * tpb, (3, 1, 1, d)
    else:
        gb = min(b, max(1, tm_cap // t))
        tm, tpb = gb * t, 1
        blk = (gb, t, d)
        hmap = lambda i: (i, 0, 0)
        mmap = lambda i: (1, i, 0, 0)
        tmap = lambda i: (0, 0)
        grid0, mblk = b // gb, (3, gb, 1, d)
    full = lambda a: pl.BlockSpec(a.shape, lambda i: (0,) * a.ndim)
    pq, pk, cq, ck = place
    ins = [h, mods, g_pre.reshape(1, d), lw["w_all"], lw["g_q"].reshape(1, -1), lw["g_kv"].reshape(1, -1),
           lw["wqn"], lw["wqr"], lw["wqs"], lw["wuk"], lw["wuv"], lw["wukt"], tables, bvec, ltri, pq, pk, cq, ck]
    in_specs = [pl.BlockSpec(blk, hmap), pl.BlockSpec(mblk, mmap)] + [full(a) for a in ins[2:12]] + \
               [pl.BlockSpec((tm, TAB_TOTAL), tmap)] + [full(a) for a in ins[13:]]
    names = ["ckv", "nsa", "win", "fox", "misc", "qn", "qx", "kx", "nq", "fq", "fkv16"]
    widths = [KV_LORA, 256, 128, 512, 128, 512, 512, 128, 256, 256, 512]
    dts = [F32] * 5 + [F32 if decode else BF16] * 6
    if decode:
        names += ["qlat", "fcum"]
        widths += [H_MLA * KV_LORA, 128]
        dts += [F32, F32]
    else:
        names += ["kn", "vm", "fqx", "fkx"]
        widths += [512, 512, 256, 256]
        dts += [BF16, BF16, BF16, BF16]
    outs = pl.pallas_call(
        functools.partial(_inproj_body, decode, tpb),
        grid=(grid0,),
        in_specs=in_specs,
        out_specs=[pl.BlockSpec((tm, w), lambda i: (i, 0)) for w in widths],
        out_shape=[jax.ShapeDtypeStruct((n, w), dt) for w, dt in zip(widths, dts)],
        scratch_shapes=[pltpu.VMEM((8, LANES), F32)],
        compiler_params=_cparams(("arbitrary",)),
        name="in_proj_dec" if decode else "in_proj",
    )(*ins)
    return dict(zip(names, outs))


def _flash_body(xw, tq, tk, nsplit, qa_ref, qx_ref, ka_ref, kx_ref, v_ref, o_ref, m_scr, l_scr, acc_scr):
    i = pl.program_id(2)
    lane = lax.broadcasted_iota(jnp.int32, (tq, LANES), 1)
    qa, qx = qa_ref[0], qx_ref[0]
    zero = jnp.zeros_like(qa)
    q0 = jnp.concatenate([jnp.where(lane < 64, qa, zero), jnp.where(lane < xw, qx, zero)], axis=1)
    q1 = jnp.concatenate([jnp.where(lane >= 64, qa, zero),
                          jnp.where((lane >= xw) & (lane < 2 * xw), qx, zero)], axis=1)
    q2 = jnp.concatenate([q0, q1], axis=0)
    m_scr[...] = jnp.full_like(m_scr, -jnp.inf)
    l_scr[...] = jnp.zeros_like(l_scr)
    acc_scr[...] = jnp.zeros_like(acc_scr)

    def chunk(c, masked):
        k0 = pl.multiple_of(c * tk, tk)
        k2 = jnp.concatenate([ka_ref[0, pl.ds(k0, tk), :], kx_ref[0, pl.ds(k0, tk), :]], axis=1)
        v = v_ref[0, pl.ds(k0, tk), :]
        tr = tq // nsplit
        if masked:
            row = lax.broadcasted_iota(jnp.int32, (tr, tk), 0)
            col = lax.broadcasted_iota(jnp.int32, (tr, tk), 1)
        for ch in range(2 * nsplit):
            rs = slice(ch * tr, (ch + 1) * tr)
            s = _dot_nt(q2[rs], k2)
            if masked:
                s = jnp.where(k0 + col <= i * tq + (ch % nsplit) * tr + row, s, -jnp.inf)
            m_prev = m_scr[rs]
            m_new = jnp.maximum(m_prev, jnp.max(s, axis=1, keepdims=True))
            alpha = jnp.exp(m_prev - m_new)
            p = jnp.exp(s - m_new[:, :1])
            l_scr[rs] = alpha * l_scr[rs] + jnp.sum(p, axis=1, keepdims=True)
            acc_scr[rs] = alpha * acc_scr[rs] + _dot(p.astype(BF16), v)
            m_scr[rs] = m_new

    n_full = (i * tq) // tk

    def body(c, carry):
        chunk(c, False)
        return carry

    lax.fori_loop(0, n_full, body, 0)
    chunk(n_full, True)
    o = acc_scr[...] * (1.0 / l_scr[...])
    o_ref[0] = jnp.where(lane < 64, o[:tq], o[tq:]).astype(o_ref.dtype)


def flash_pairs(qa, qx, ka, kx, v, *, xw, kx_shared, v_off=0, tq=512, tk=512):
    b, t, w = qa.shape
    npair = w // LANES
    tq, tk = min(tq, t), min(tk, t)
    kxmap = (lambda bb, p, i: (bb, 0, 0)) if kx_shared else (lambda bb, p, i: (bb, 0, p))
    return pl.pallas_call(
        functools.partial(_flash_body, xw, tq, tk, 1),
        grid=(b, npair, t // tq),
        in_specs=[pl.BlockSpec((1, tq, LANES), lambda bb, p, i: (bb, i, p)),
                  pl.BlockSpec((1, tq, LANES), lambda bb, p, i: (bb, i, p)),
                  pl.BlockSpec((1, t, LANES), lambda bb, p, i: (bb, 0, p)),
                  pl.BlockSpec((1, t, LANES), kxmap),
                  pl.BlockSpec((1, t, LANES), lambda bb, p, i: (bb, 0, v_off + p))],
        out_specs=pl.BlockSpec((1, tq, LANES), lambda bb, p, i: (bb, i, p)),
        out_shape=jax.ShapeDtypeStruct((b, t, w), BF16),
        scratch_shapes=[pltpu.VMEM((2 * tq, LANES), F32)] * 3,
        compiler_params=_cparams(("arbitrary", "arbitrary", "arbitrary")),
        name="flash_pairs",
    )(qa, qx, ka, kx, v)


def _compress_body(x_ref, w_ref, o_ref):
    @pl.when(pl.program_id(0) == 0)
    def _():
        o_ref[...] = jnp.zeros_like(o_ref)

    o_ref[...] += _dot(x_ref[...].astype(BF16), w_ref[...])


def _cmp_weight(w_cmp):
    z = jnp.zeros((L_CMP, NSA_DIM, NSA_DIM), w_cmp.dtype)
    top = jnp.concatenate([w_cmp[0], z], axis=2)
    bot = jnp.concatenate([z, w_cmp[1]], axis=2)
    zz = jnp.zeros((L_CMP, 2 * NSA_DIM, 2 * NSA_DIM), w_cmp.dtype)
    return jnp.concatenate([top, bot, zz], axis=1).reshape(L_CMP * 4 * NSA_DIM, 2 * NSA_DIM).astype(BF16)


def compress_blocks(nsa_rows, wbig, *, tk=2048):
    nblk, kk = nsa_rows.shape
    tk = min(tk, kk)
    return pl.pallas_call(
        _compress_body,
        grid=(kk // tk,),
        in_specs=[pl.BlockSpec((nblk, tk), lambda k: (0, k)), pl.BlockSpec((tk, LANES), lambda k: (k, 0))],
        out_specs=pl.BlockSpec((nblk, LANES), lambda k: (0, 0)),
        out_shape=jax.ShapeDtypeStruct((nblk, LANES), F32),
        compiler_params=_cparams(("arbitrary",)),
        name="compress_blocks",
    )(nsa_rows, wbig)


def _head_slabs(qf, lane):
    slabs = []
    for p in range(H_NSA // 2):
        x = qf[:, p * LANES:(p + 1) * LANES]
        slabs.append(jnp.where(lane < 64, x, 0.0))
        slabs.append(jnp.where(lane < 64, pltpu.roll(x, 64, 1), 0.0))
    return jnp.concatenate(slabs, axis=0)


def _select_blocks(imp, valid, bid, lane_bids, n_sel):
    rank = jnp.zeros(imp.shape, F32)
    for j, bj in lane_bids:
        col = imp[:, j:j + 1]
        ahead = (col > imp) | ((col == imp) & (bid > bj))
        rank = rank + jnp.where(ahead, 1.0, 0.0)
    return jnp.where((rank < n_sel) & valid, 1.0, 0.0)


def _nsa_prompt_body(tq, t, nb, span, n_sel, kstep, *refs):
    i = pl.program_id(1)
    need = ((i + 1) * tq + kstep - 1) // kstep
    for v in range(1, t // kstep + 1):
        @pl.when(need == v)
        def _():
            _nsa_prompt_tile(v * kstep, tq, t, nb, span, n_sel, i, *refs)


def _nsa_prompt_tile(klen, tq, t, nb, span, n_sel, i, q_ref, ckv_ref, kv_ref, win_ref, misc_ref, e_ref, o_ref):
    lane = lax.broadcasted_iota(jnp.int32, (tq, LANES), 1)
    q = _head_slabs(q_ref[0].astype(F32), lane).astype(BF16)
    rows = H_NSA * tq
    row4 = lax.broadcasted_iota(jnp.int32, (rows, LANES), 0)
    qpos4 = i * tq + (row4 & (tq - 1))
    bi4 = lax.broadcasted_iota(jnp.int32, (rows, LANES), 1)
    ckv16 = ckv_ref[0].astype(BF16)
    done = ((bi4 + 1) * L_CMP - 1 <= qpos4) & (bi4 < nb)
    pc = _softmax_rows(jnp.where(done, _dot_nt(q, ckv16), -jnp.inf))
    oc = _dot(pc.astype(BF16), ckv16)
    imp = pc[0:tq] + pc[tq:2 * tq] + pc[2 * tq:3 * tq] + pc[3 * tq:4 * tq]
    qpos1 = i * tq + lax.broadcasted_iota(jnp.int32, (tq, LANES), 0)
    cur = qpos1 // L_CMP
    forced = (lane == 0) | (lane == cur) | (lane == cur - 1)
    valid = (lane <= cur) & (lane < nb)
    imp = jnp.where(valid, jnp.where(forced, FORCED_SCORE, imp), -jnp.inf)
    sel = _select_blocks(imp, valid, lane, tuple((j, j) for j in range(nb)), n_sel)
    seltok = _dot(sel.astype(BF16), e_ref[:, :klen])
    kpos = lax.broadcasted_iota(jnp.int32, (tq, klen), 1)
    qpos_t = i * tq + lax.broadcasted_iota(jnp.int32, (tq, klen), 0)
    bias = jnp.where((seltok > 0.5) & (kpos <= qpos_t), 0.0, -jnp.inf)
    kv16 = kv_ref[0, :klen, :].astype(BF16)
    ps = _softmax_rows(_dot_nt(q, kv16) + jnp.concatenate([bias] * H_NSA, axis=0))
    osel = _dot(ps.astype(BF16), kv16)
    start = jnp.clip(i * tq - WINDOW, 0, t - span)
    start = pl.multiple_of(start, 8)
    w16 = win_ref[0, pl.ds(start, span), :].astype(BF16)
    roww = lax.broadcasted_iota(jnp.int32, (rows, span), 0)
    diff = i * tq + (roww & (tq - 1)) - (start + lax.broadcasted_iota(jnp.int32, (rows, span), 1))
    pw = _softmax_rows(jnp.where((diff >= 0) & (diff <= WINDOW), _dot_nt(q, w16), -jnp.inf))
    ow = _dot(pw.astype(BF16), w16)
    g = misc_ref[0]
    heads = []
    for h in range(H_NSA):
        sl = slice(h * tq, (h + 1) * tq)
        c0 = MISC_GATE + 3 * h
        heads.append(g[:, c0:c0 + 1] * oc[sl] + g[:, c0 + 1:c0 + 2] * osel[sl] + g[:, c0 + 2:c0 + 3] * ow[sl])
    pairs = [jnp.where(lane < 64, pltpu.roll(heads[2 * p], 64, 1), heads[2 * p + 1]) for p in range(H_NSA // 2)]
    o_ref[0] = jnp.concatenate(pairs, axis=1).astype(o_ref.dtype)


def nsa_prompt(nq, cmpkv, nsa, win, misc, *, tq=128):
    b, t, _ = nq.shape
    nb = -(-t // L_CMP)
    tq = min(tq, t)
    span = min(t, WINDOW + tq)
    n_sel = min(N_SEL, nb)
    e = (jnp.arange(t)[None, :] // L_CMP == jnp.arange(LANES)[:, None]).astype(BF16)
    kstep = min(t, max(tq, 512))
    return pl.pallas_call(
        functools.partial(_nsa_prompt_body, tq, t, nb, span, n_sel, kstep),
        grid=(b, t // tq),
        in_specs=[pl.BlockSpec((1, tq, 256), lambda bb, i: (bb, i, 0)),
                  pl.BlockSpec((1, LANES, LANES), lambda bb, i: (bb, 0, 0)),
                  pl.BlockSpec((1, t, LANES), lambda bb, i: (bb, 0, 1)),
                  pl.BlockSpec((1, t, LANES), lambda bb, i: (bb, 0, 0)),
                  pl.BlockSpec((1, tq, LANES), lambda bb, i: (bb, i, 0)),
                  pl.BlockSpec((LANES, t), lambda bb, i: (0, 0))],
        out_specs=pl.BlockSpec((1, tq, 256), lambda bb, i: (bb, i, 0)),
        out_shape=jax.ShapeDtypeStruct((b, t, 256), BF16),
        compiler_params=_cparams(("arbitrary", "arbitrary")),
        name="nsa_prompt",
    )(nq, cmpkv, nsa, win, misc, e)


def _outproj_body(decode, *refs):
    if decode:
        ol_ref, wuvb_ref, on_ref, of_ref, w_ref, h_ref, m_ref, gpost_ref, o_ref = refs
        om = _dot(ol_ref[...].astype(BF16), wuvb_ref[...]).astype(BF16)
    else:
        om_ref, on_ref, of_ref, w_ref, h_ref, m_ref, gpost_ref, o_ref = refs
        om = om_ref[...]
    g_, r_, d_ = h_ref.shape
    o = jnp.concatenate([om, on_ref[...].astype(BF16), of_ref[...].astype(BF16)], axis=1)
    yn = _rms(_dot(o, w_ref[...]), gpost_ref[...]).reshape(g_, r_, d_)
    o_ref[...] = h_ref[...] + m_ref[2] * yn


def out_proj(o_mla, o_nsa, o_fox, w_out, h, mods, g_post, *, wuv_bd=None, tm_cap=512):
    b, t, d = h.shape
    decode = wuv_bd is not None
    if not decode:
        tm = _tok_tile(t, tm_cap)
        tpb = t // tm
        blk = (1, tm, d)
        hmap = lambda i: (i // tpb, i % tpb, 0)
        mmap = lambda i: (1, i // tpb, 0, 0)
        grid0, mblk = b * tpb, (3, 1, 1, d)
    else:
        gb = min(b, max(1, tm_cap // t))
        tm = gb * t
        blk = (gb, t, d)
        hmap = lambda i: (i, 0, 0)
        mmap = lambda i: (1, i, 0, 0)
        grid0, mblk = b // gb, (3, gb, 1, d)
    row = lambda a: pl.BlockSpec((tm, a.shape[1]), lambda i: (i, 0))
    full = lambda a: pl.BlockSpec(a.shape, lambda i: (0,) * a.ndim)
    ins = [o_mla] + ([wuv_bd] if decode else []) + [o_nsa, o_fox, w_out, h, mods, g_post.reshape(1, d)]
    specs = [row(o_mla)] + ([full(wuv_bd)] if decode else []) + \
            [row(o_nsa), row(o_fox), full(w_out), pl.BlockSpec(blk, hmap), pl.BlockSpec(mblk, mmap),
             pl.BlockSpec((1, d), lambda i: (0, 0))]
    return pl.pallas_call(
        functools.partial(_outproj_body, decode),
        grid=(grid0,),
        in_specs=specs,
        out_specs=pl.BlockSpec(blk, hmap),
        out_shape=jax.ShapeDtypeStruct(h.shape, F32),
        compiler_params=_cparams(("arbitrary",)),
        name="out_proj_dec" if decode else "out_proj",
    )(*ins)


NEW_PAD = LANES


def _page_copy(spec, layer, page, slot, j):
    cache_ref, buf, sem, src_rows, mode, width = spec
    src = cache_ref.at[layer, page]
    if src_rows is not None:
        src = src.at[pl.ds(src_rows[0], src_rows[1])]
    if mode == "lanes":
        dst = buf.at[slot, :, pl.ds(j * width, width)]
    elif mode == "mid":
        dst = buf.at[slot, :, j, :]
    else:
        dst = buf.at[slot, pl.ds(j * width, width)]
    return pltpu.make_async_copy(src, dst, sem.at[slot])


def _fetch_pages(pt_ref, bb, slot, layer, n_pages, specs):
    def body(j, c):
        page = pt_ref[bb, j]
        for spec in specs:
            _page_copy(spec, layer, page, slot, j).start()
        return c

    lax.fori_loop(0, n_pages, body, 0, unroll=min(8, n_pages))


def _wait_pages(slot, layer, n_pages, specs):
    for j in range(n_pages):
        for spec in specs:
            _page_copy(spec, layer, 0, slot, j).wait()


def _gather_step(pt_ref, layer, n_pages, specs):
    b = pl.program_id(0)
    slot = b % 2

    @pl.when(b == 0)
    def _():
        _fetch_pages(pt_ref, 0, 0, layer, n_pages, specs)

    @pl.when(b + 1 < pl.num_programs(0))
    def _():
        _fetch_pages(pt_ref, b + 1, 1 - slot, layer, n_pages, specs)

    _wait_pages(slot, layer, n_pages, specs)
    return slot


def _pad_new(x):
    dt, w = x.shape
    return jnp.concatenate([x, jnp.zeros((NEW_PAD - dt, w), x.dtype)], axis=0).astype(BF16)


def _new_bias(rows, dt, ok=None):
    r = lax.broadcasted_iota(jnp.int32, (rows, NEW_PAD), 0)
    c = lax.broadcasted_iota(jnp.int32, (rows, NEW_PAD), 1)
    vis = c <= (r % dt)
    if ok is not None:
        vis = vis & ok
    return jnp.where(vis, 0.0, -jnp.inf)


def _joint_softmax_pv(sa, sb, va, vb, va_t=False):
    m = jnp.maximum(jnp.max(sa, axis=-1, keepdims=True), jnp.max(sb, axis=-1, keepdims=True))
    m = jnp.where(m > -jnp.inf, m, 0.0)
    ea, eb = jnp.exp(sa - m), jnp.exp(sb - m)
    d = jnp.sum(ea, axis=-1, keepdims=True) + jnp.sum(eb, axis=-1, keepdims=True)
    oa = _dot_nt(ea.astype(BF16), va) if va_t else _dot(ea.astype(BF16), va)
    o = oa + _dot(eb.astype(BF16), vb)
    return o * (1.0 / jnp.where(d > 0, d, 1.0))


def _mla_dec_body(layer, n_pages, page, pt_ref, qlat_ref, qx_ref, ckvn_ref, misc_ref, ckv_hbm, kr_hbm, o_ref,
                  ckv_buf, kr_buf, sem_c, sem_r):
    dt = qlat_ref.shape[1]
    slot = _gather_step(pt_ref, layer, n_pages, [(ckv_hbm, ckv_buf, sem_c, None, "rows", page),
                                                 (kr_hbm, kr_buf, sem_r, None, "lanes", page)])
    ql = qlat_ref[0]
    q = jnp.concatenate([ql[:, h * KV_LORA:(h + 1) * KV_LORA] for h in range(H_MLA)], axis=0).astype(BF16)
    lane = lax.broadcasted_iota(jnp.int32, (dt, LANES), 1)
    qxf = qx_ref[0]
    parts = []
    for h in range(H_MLA):
        x = qxf[:, (h // 2) * LANES:(h // 2 + 1) * LANES]
        if h % 2:
            x = pltpu.roll(x, LANES - ROPE_DIM, 1)
        parts.append(jnp.where(lane < ROPE_DIM, x, 0.0))
    qr = jnp.concatenate(parts, axis=0)[:, :ROPE_DIM].astype(BF16)
    ckv16 = ckv_buf[slot].astype(BF16)
    krt16 = kr_buf[slot].astype(BF16)
    ckvn16 = _pad_new(ckvn_ref[0])
    krn16 = _pad_new(misc_ref[0][:, :ROPE_DIM])
    sa = _dot_nt(q, ckv16) + _dot(qr, krt16)
    sb = _dot_nt(q, ckvn16) + _dot_nt(qr, krn16) + _new_bias(H_MLA * dt, dt)
    o = _joint_softmax_pv(sa, sb, ckv16, ckvn16)
    for h in range(H_MLA):
        o_ref[0, :, h * KV_LORA:(h + 1) * KV_LORA] = o[h * dt:(h + 1) * dt].astype(o_ref.dtype)


def mla_decode(layer, page_table, qlat, qx, ckv_new, misc, cache_ckv, cache_krt):
    db, dt, _ = qlat.shape
    n_pages, page = page_table.shape[1], cache_ckv.shape[2]
    past = n_pages * page
    blk = lambda w: pl.BlockSpec((1, dt, w), lambda b, pt: (b, 0, 0))
    any_spec = pl.BlockSpec(memory_space=pl.ANY)
    return pl.pallas_call(
        functools.partial(_mla_dec_body, layer, n_pages, page),
        grid_spec=pltpu.PrefetchScalarGridSpec(
            num_scalar_prefetch=1, grid=(db,),
            in_specs=[blk(H_MLA * KV_LORA), blk(qx.shape[-1]), blk(KV_LORA), blk(LANES), any_spec, any_spec],
            out_specs=blk(H_MLA * KV_LORA),
            scratch_shapes=[pltpu.VMEM((2, past, KV_LORA), F32), pltpu.VMEM((2, ROPE_DIM, past), F32),
                            pltpu.SemaphoreType.DMA((2,)), pltpu.SemaphoreType.DMA((2,))]),
        out_shape=jax.ShapeDtypeStruct((db, dt, H_MLA * KV_LORA), F32),
        compiler_params=_cparams(("arbitrary",)),
        name="mla_decode",
    )(page_table, qlat, qx, ckv_new, misc, cache_ckv, cache_krt)


def _fox_dec_body(layer, n_pages, page, pt_ref, fq_ref, fnew_ref, fcum_ref, kv_hbm, lf_hbm, o_ref,
                  kt_buf, vt_buf, lf_buf, sem_k, sem_v, sem_l):
    dt = fq_ref.shape[1]
    rows = H_FOX * dt
    hw = H_FOX * FOX_DIM
    slot = _gather_step(pt_ref, layer, n_pages, [(kv_hbm, kt_buf, sem_k, (0, hw), "lanes", page),
                                                 (kv_hbm, vt_buf, sem_v, (hw, hw), "lanes", page),
                                                 (lf_hbm, lf_buf, sem_l, None, "rows", H_FOX)])
    lane = lax.broadcasted_iota(jnp.int32, (dt, hw), 1)
    qf = fq_ref[0]
    q = jnp.concatenate([jnp.where(lane // FOX_DIM == h, qf, 0.0) for h in range(H_FOX)], axis=0).astype(BF16)
    kt16 = kt_buf[slot].astype(BF16)
    vt16 = vt_buf[slot].astype(BF16)
    fnew = fnew_ref[0]
    kn16, vn16 = _pad_new(fnew[:, :hw]), _pad_new(fnew[:, hw:])
    nr = n_pages * H_FOX
    lf = lf_buf[slot]
    ri = lax.broadcasted_iota(jnp.int32, (page, page), 0)
    ci = lax.broadcasted_iota(jnp.int32, (page, page), 1)
    tri = jnp.where(ri > ci, 1.0, 0.0).astype(BF16)
    ones = jnp.ones((page, page), BF16)
    pr = lax.broadcasted_iota(jnp.int32, (nr, nr), 0)
    pc = lax.broadcasted_iota(jnp.int32, (nr, nr), 1)
    upper = jnp.where((pc % H_FOX == pr % H_FOX) & (pc // H_FOX > pr // H_FOX), 1.0, 0.0).astype(BF16)
    l3 = _split3(lf)
    within = _dot(l3[0], tri) + _dot(l3[1], tri) + _dot(l3[2], tri)
    tot = _dot(l3[0], ones) + _dot(l3[1], ones) + _dot(l3[2], ones)
    t3 = _split3(tot)
    r2 = within + _dot(upper, t3[0]) + _dot(upper, t3[1]) + _dot(upper, t3[2])
    bias_rows = []
    for h in range(H_FOX):
        flat = jnp.concatenate([r2[j * H_FOX + h:j * H_FOX + h + 1, :] for j in range(n_pages)], axis=1)
        bias_rows.append(jnp.broadcast_to(flat, (dt, flat.shape[1])))
    fc = fcum_ref[0]
    gt_col = jnp.concatenate([fc[:, MISC_LOGF + h:MISC_LOGF + h + 1] for h in range(H_FOX)], axis=0)
    sa = _dot(q, kt16) + jnp.concatenate(bias_rows, axis=0) + gt_col
    lane_r = lax.broadcasted_iota(jnp.int32, (rows, LANES), 1)
    row_r = lax.broadcasted_iota(jnp.int32, (rows, LANES), 0)
    onehot = jnp.where(lane_r == MISC_LOGF + row_r // dt, 1.0, 0.0).astype(BF16)
    f3 = _split3(jnp.concatenate([fc, jnp.zeros((NEW_PAD - dt, LANES), F32)], axis=0))
    gt_row = _dot_nt(onehot, f3[0]) + _dot_nt(onehot, f3[1]) + _dot_nt(onehot, f3[2])
    sb = _dot_nt(q, kn16) + gt_col - gt_row + _new_bias(rows, dt)
    o = _joint_softmax_pv(sa, sb, vt16, vn16, va_t=True)
    out = jnp.zeros((dt, hw), F32)
    for h in range(H_FOX):
        out = out + jnp.where(lane // FOX_DIM == h, o[h * dt:(h + 1) * dt], 0.0)
    o_ref[0] = out.astype(o_ref.dtype)


def fox_decode(layer, page_table, fq, fox_new, fcum, cache_kvt, cache_lft):
    db, dt, hw = fq.shape
    n_pages, page = page_table.shape[1], cache_kvt.shape[3]
    past = n_pages * page
    blk = lambda w: pl.BlockSpec((1, dt, w), lambda b, pt: (b, 0, 0))
    any_spec = pl.BlockSpec(memory_space=pl.ANY)
    return pl.pallas_call(
        functools.partial(_fox_dec_body, layer, n_pages, page),
        grid_spec=pltpu.PrefetchScalarGridSpec(
            num_scalar_prefetch=1, grid=(db,),
            in_specs=[blk(hw), blk(2 * hw), blk(LANES), any_spec, any_spec],
            out_specs=blk(hw),
            scratch_shapes=[pltpu.VMEM((2, hw, past), F32), pltpu.VMEM((2, hw, past), F32),
                            pltpu.VMEM((2, n_pages * H_FOX, page), F32),
                            pltpu.SemaphoreType.DMA((2,)), pltpu.SemaphoreType.DMA((2,)),
                            pltpu.SemaphoreType.DMA((2,))]),
        out_shape=jax.ShapeDtypeStruct((db, dt, hw), F32),
        compiler_params=_cparams(("arbitrary",)),
        name="fox_decode",
    )(page_table, fq, fox_new, fcum, cache_kvt, cache_lft)


def _cmp_weight_pairs(w_cmp):
    z = jnp.zeros((L_CMP, NSA_DIM, NSA_DIM), w_cmp.dtype)
    top = jnp.concatenate([w_cmp[0], z], axis=2)
    bot = jnp.concatenate([z, w_cmp[1]], axis=2)
    return jnp.concatenate([top, bot], axis=1).reshape(L_CMP * LANES, LANES).astype(BF16)


def _cmp_weight_paged(w, bpp):
    eye = jnp.eye(bpp, dtype=w.dtype)
    v = jnp.einsum("lde,pq->dplqe", w, eye)
    return v.reshape(NSA_DIM * bpp * L_CMP, bpp * NSA_DIM).astype(BF16)


def _nsa_dec_body(layer, n_pages, page, n_sel, lane_bids, pt_ref, nq_ref, new_ref, wnew_ref, misc_ref, wpast_ref,
                  wl_ref, vk_ref, vv_ref, e_ref, bid_ref, nsa_hbm, o_ref, cmp_buf, sel_buf, xk_scr, xv_scr,
                  sem_c, sem_s):
    dt = nq_ref.shape[1]
    past = n_pages * page
    nbp = past // L_CMP
    nb = nbp + 1
    nbl = e_ref.shape[0]
    rows = H_NSA * dt
    d2 = 2 * NSA_DIM
    slot = _gather_step(pt_ref, layer, n_pages, [(nsa_hbm, cmp_buf, sem_c, (0, d2), "mid", None),
                                                 (nsa_hbm, sel_buf, sem_s, (d2, d2), "lanes", page)])
    lane = lax.broadcasted_iota(jnp.int32, (dt, LANES), 1)
    q = _head_slabs(nq_ref[0], lane)[:, :NSA_DIM].astype(BF16)
    new = new_ref[0]
    for d in range(NSA_DIM):
        xk_scr[:, d * page:(d + 1) * page] = cmp_buf[slot, d].astype(BF16)
        xv_scr[:, d * page:(d + 1) * page] = cmp_buf[slot, NSA_DIM + d].astype(BF16)
    sk = _dot(xk_scr[...], vk_ref[...])
    sv = _dot(xv_scr[...], vv_ref[...])
    part = jnp.zeros((1, LANES), F32)
    for l in range(dt):
        part = part + _dot(jnp.broadcast_to(new[l:l + 1, :LANES], (8, LANES)).astype(BF16),
                           wl_ref[l * LANES:(l + 1) * LANES, :])[0:1]
    tail_row = lax.broadcasted_iota(jnp.int32, (nbl - nbp, LANES), 0)
    tail = jnp.where(tail_row == 0, jnp.broadcast_to(part, (nbl - nbp, LANES)), 0.0)
    kc16 = jnp.concatenate([sk[:, :NSA_DIM], sk[:, NSA_DIM:], tail[:, :NSA_DIM]], axis=0).astype(BF16)
    vc16 = jnp.concatenate([sv[:, :NSA_DIM], sv[:, NSA_DIM:], tail[:, NSA_DIM:]], axis=0).astype(BF16)
    bid4 = jnp.broadcast_to(bid_ref[...], (rows, nbl))
    qpos4 = past + lax.broadcasted_iota(jnp.int32, (rows, nbl), 0) % dt
    done = ((bid4 + 1) * L_CMP - 1 <= qpos4) & (bid4 < nb)
    pc = _softmax_rows(jnp.where(done, _dot_nt(q, kc16), -jnp.inf))
    oc = _dot(pc.astype(BF16), vc16)
    imp = pc[0:dt]
    for h in range(1, H_NSA):
        imp = imp + pc[h * dt:(h + 1) * dt]
    bid = jnp.broadcast_to(bid_ref[...], (dt, nbl))
    cur = (past + lax.broadcasted_iota(jnp.int32, (dt, nbl), 0)) // L_CMP
    forced = (bid == 0) | (bid == cur) | (bid == cur - 1)
    valid = (bid <= cur) & (bid < nb)
    imp = jnp.where(valid, jnp.where(forced, FORCED_SCORE, imp), -jnp.inf)
    sel = _select_blocks(imp, valid, bid, lane_bids, n_sel)
    seltok = _dot(sel.astype(BF16), e_ref[...])
    bias = jnp.where(seltok > 0.5, 0.0, -jnp.inf)
    kvt16 = sel_buf[slot].astype(BF16)
    kn16 = _pad_new(new[:, 2 * NSA_DIM:3 * NSA_DIM])
    vn16 = _pad_new(new[:, 3 * NSA_DIM:])
    new_lane = 2 * (nbp // 2) if nbp % 2 == 0 else None
    sel_new = jnp.concatenate([sel[:, new_lane:new_lane + 1]] * H_NSA, axis=0) > 0.5
    sa = _dot(q, kvt16[:NSA_DIM]) + jnp.concatenate([bias] * H_NSA, axis=0)
    sb = _dot_nt(q, kn16) + _new_bias(rows, dt, sel_new)
    osel = _joint_softmax_pv(sa, sb, kvt16[NSA_DIM:], vn16, va_t=True)
    wpt16 = wpast_ref[0, 0].astype(BF16)
    wn = wnew_ref[0]
    wb = wpt16.shape[1]
    qpw = past + lax.broadcasted_iota(jnp.int32, (rows, wb), 0) % dt
    kpw = past - wb + lax.broadcasted_iota(jnp.int32, (rows, wb), 1)
    okw = (qpw - kpw <= WINDOW) & (kpw >= 0)
    sa = jnp.where(okw, _dot(q, wpt16[:NSA_DIM]), -jnp.inf)
    sb = _dot_nt(q, _pad_new(wn[:, :NSA_DIM])) + _new_bias(rows, dt)
    ow = _joint_softmax_pv(sa, sb, wpt16[NSA_DIM:], _pad_new(wn[:, NSA_DIM:]), va_t=True)
    g = misc_ref[0]
    heads = []
    for h in range(H_NSA):
        sl = slice(h * dt, (h + 1) * dt)
        c0 = MISC_GATE + 3 * h
        heads.append(g[:, c0:c0 + 1] * oc[sl] + g[:, c0 + 1:c0 + 2] * osel[sl] + g[:, c0 + 2:c0 + 3] * ow[sl])
    o_ref[0] = jnp.concatenate(heads, axis=1).astype(o_ref.dtype)


def nsa_decode(layer, page_table, nq, nsa_new, win_new, misc, win_t, wl, vk, vv, cache_nsat):
    db, dt, _ = nq.shape
    n_pages, page = page_table.shape[1], cache_nsat.shape[3]
    past = n_pages * page
    assert page == 2 * L_CMP and dt <= min(L_CMP, WINDOW, NEW_PAD)
    wb = win_t.shape[3]
    nbp = past // L_CMP
    nb = nbp + 1
    nbl = -(-nb // LANES) * LANES
    n_sel = min(N_SEL, nb)
    bids = np.full((nbl,), nbl + nb, np.int32)
    bids[:n_pages] = 2 * np.arange(n_pages)
    bids[n_pages:2 * n_pages] = 2 * np.arange(n_pages) + 1
    bids[2 * n_pages] = nbp
    lane_bids = tuple((int(i), int(bids[i])) for i in range(2 * n_pages + 1))
    e = (jnp.arange(past)[None, :] // L_CMP == jnp.asarray(bids)[:, None]).astype(BF16)
    blk = lambda w: pl.BlockSpec((1, dt, w), lambda b, pt: (b, 0, 0))
    full2 = lambda a: pl.BlockSpec(a.shape, lambda b, pt: (0, 0))
    bid_arr = jnp.asarray(bids).reshape(1, nbl)
    return pl.pallas_call(
        functools.partial(_nsa_dec_body, layer, n_pages, page, n_sel, lane_bids),
        grid_spec=pltpu.PrefetchScalarGridSpec(
            num_scalar_prefetch=1, grid=(db,),
            in_specs=[blk(256), blk(256), blk(LANES), blk(LANES),
                      pl.BlockSpec((1, 1, 2 * NSA_DIM, wb), lambda b, pt: (layer, b, 0, 0)),
                      full2(wl), full2(vk), full2(vv), full2(e), full2(bid_arr),
                      pl.BlockSpec(memory_space=pl.ANY)],
            out_specs=blk(256),
            scratch_shapes=[pltpu.VMEM((2, 2 * NSA_DIM, n_pages, page), F32), pltpu.VMEM((2, 2 * NSA_DIM, past), F32),
                            pltpu.VMEM((n_pages, NSA_DIM * page), BF16), pltpu.VMEM((n_pages, NSA_DIM * page), BF16),
                            pltpu.SemaphoreType.DMA((2,)), pltpu.SemaphoreType.DMA((2,))]),
        out_shape=jax.ShapeDtypeStruct((db, dt, 256), F32),
        compiler_params=_cparams(("arbitrary",)),
        name="nsa_decode",
    )(page_table, nq, nsa_new, win_new, misc, win_t, wl, vk, vv, e, bid_arr, cache_nsat)


def _mixer_prompt(h, mods, g_pre, g_post, lw, consts):
    b, t, d = h.shape
    z = in_proj(h, mods, g_pre, lw, consts["tab_p"], lw["bvec"], consts["ltri_p"], consts["place"], decode=False)
    r3 = lambda a: a.reshape(b, t, a.shape[-1])
    o_mla = flash_pairs(r3(z["qn"]), r3(z["qx"]), r3(z["kn"]), r3(z["kx"]), r3(z["vm"]),
                        xw=ROPE_DIM, kx_shared=True)
    fkv16 = r3(z["fkv16"])
    o_fox = flash_pairs(r3(z["fq"]), r3(z["fqx"]), fkv16, r3(z["fkx"]), fkv16,
                        xw=FOX_XW, kx_shared=False, v_off=H_FOX // 2)
    nb = t // L_CMP
    cmpkv = compress_blocks(z["nsa"].reshape(b * nb, L_CMP * 256), lw["wbig"]).reshape(b, nb, LANES)
    cmpkv = jnp.pad(cmpkv, ((0, 0), (0, LANES - nb), (0, 0)))
    o_nsa = nsa_prompt(r3(z["nq"]), cmpkv, r3(z["nsa"]), r3(z["win"]), r3(z["misc"]))
    n = b * t
    h_new = out_proj(o_mla.reshape(n, -1), o_nsa.reshape(n, -1), o_fox.reshape(n, -1), lw["w_out"], h, mods, g_post)
    wb = min(WINDOW, t)
    misc = r3(z["misc"])
    state = (r3(z["ckv"]), misc[:, :, :ROPE_DIM], r3(z["nsa"]).reshape(b, t, 4, NSA_DIM),
             r3(z["win"])[:, t - wb:].reshape(b, wb, 2, NSA_DIM),
             r3(z["fox"]).reshape(b, t, 2, H_FOX, FOX_DIM), misc[:, :, MISC_LOGF:MISC_END])
    return h_new, state


def _mixer_sample(h, mods, g_pre, g_post, lw, consts, l, page_table, c_ckv, c_kr, c_nsa, s_win, c_fkv, c_flf):
    db, dt, d = h.shape
    z = in_proj(h, mods, g_pre, lw, consts["tab_s"], lw["bvec"], consts["ltri_s"], consts["place"], decode=True)
    r3 = lambda a: a.reshape(db, dt, a.shape[-1])
    misc = r3(z["misc"])
    o_lat = mla_decode(l, page_table, r3(z["qlat"]), r3(z["qx"]), r3(z["ckv"]), misc, c_ckv, c_kr)
    o_nsa = nsa_decode(l, page_table, r3(z["nq"]), r3(z["nsa"]), r3(z["win"]), misc, consts["win_t"], lw["wl"],
                       lw["vk"], lw["vv"], c_nsa)
    o_fox = fox_decode(l, page_table, r3(z["fq"]), r3(z["fox"]), r3(z["fcum"]), c_fkv, c_flf)
    win_new = jnp.concatenate([s_win[l, :, dt:], r3(z["win"]).reshape(db, dt, 2, NSA_DIM)], axis=1)
    n = db * dt
    h_new = out_proj(o_lat.reshape(n, -1), o_nsa.reshape(n, -1), o_fox.reshape(n, -1), lw["w_out"], h, mods, g_post,
                     wuv_bd=lw["wuv_bd"])
    state = (r3(z["ckv"]), misc[:, :, :ROPE_DIM], r3(z["nsa"]).reshape(db, dt, 4, NSA_DIM),
             win_new, r3(z["fox"]).reshape(db, dt, 2, H_FOX, FOX_DIM),
             misc[:, :, MISC_LOGF:MISC_END])
    return h_new, state


def _layer_weights(l, w_in, b_fox_f, mla_g_q, mla_g_kv, mla_w_uq, mla_w_uk, mla_w_uv, nsa_w_cmp, w_out):
    lw = _inproj_weights(w_in[l], mla_w_uq[l], mla_w_uk[l], mla_w_uv[l])
    bvec = jnp.zeros((1, LANES), F32).at[0, MISC_LOGF:MISC_END].set(b_fox_f[l])
    eye = jnp.eye(H_MLA, dtype=F32)
    wuv_bd = jnp.einsum("chd,hg->hcgd", mla_w_uv[l], eye).reshape(H_MLA * KV_LORA, H_MLA * V_DIM)
    lw.update(g_q=mla_g_q[l], g_kv=mla_g_kv[l], bvec=bvec, wbig=_cmp_weight(nsa_w_cmp[l]),
              w_out=w_out[l].astype(BF16), wuv_bd=wuv_bd.astype(BF16), wl=_cmp_weight_pairs(nsa_w_cmp[l]),
              vk=_cmp_weight_paged(nsa_w_cmp[l, 0], 2), vv=_cmp_weight_paged(nsa_w_cmp[l, 1], 2))
    return lw


def kernel(x_prompt, x_sample, cache_mla_ckv, cache_mla_krope, cache_nsa_kv, state_nsa_win, cache_fox_kv,
           cache_fox_logf, page_table, c_prompt, c_sample, ada_w, ada_b, norm_pre, norm_post, ffn_w_gate,
           ffn_w_up, ffn_w_down, w_in, b_fox_f, mla_g_q, mla_g_kv, mla_w_uq, mla_w_uk, mla_w_uv, nsa_w_cmp, w_out):
    depth = w_in.shape[0]
    b, t, d = x_prompt.shape
    db, dt, _ = x_sample.shape
    past = page_table.shape[1] * cache_mla_ckv.shape[2]
    mods_all = ada_mods(jnp.concatenate([c_prompt, c_sample], axis=0), ada_w, ada_b)
    wg16, wu16, wd16 = ffn_w_gate.astype(BF16), ffn_w_up.astype(BF16), ffn_w_down.astype(BF16)
    tm_p = min(t, 512)
    gb = min(db, max(1, 512 // dt))
    tm_s = gb * dt
    ii = jnp.arange(tm_s)
    consts = dict(
        tab_p=_rope_tables(jnp.arange(t, dtype=jnp.int32)),
        tab_s=jnp.tile(_rope_tables(past + jnp.arange(dt, dtype=jnp.int32)), (gb, 1)),
        ltri_p=(jnp.arange(tm_p)[:, None] >= jnp.arange(tm_p)[None, :]).astype(BF16),
        ltri_s=((ii[:, None] >= ii[None, :]) & (ii[:, None] // dt == ii[None, :] // dt)).astype(BF16),
        place=_fox_place(),
        win_t=jnp.transpose(state_nsa_win, (0, 1, 3, 4, 2)).reshape(depth, db, 2 * NSA_DIM, -1))
    pool, page = cache_mla_ckv.shape[1:3]
    cache_krt = jnp.transpose(cache_mla_krope, (0, 1, 3, 2))
    cache_nsat = jnp.transpose(cache_nsa_kv, (0, 1, 3, 4, 2)).reshape(depth, pool, 4 * NSA_DIM, page)
    cache_fkvt = jnp.transpose(cache_fox_kv, (0, 1, 3, 4, 5, 2)).reshape(depth, pool, 2 * H_FOX * FOX_DIM, page)
    cache_lft = jnp.transpose(cache_fox_logf, (0, 1, 3, 2))
    hp, hs = x_prompt, x_sample
    st_p, st_s = [], []
    for l in range(depth):
        lw = _layer_weights(l, w_in, b_fox_f, mla_g_q, mla_g_kv, mla_w_uq, mla_w_uk, mla_w_uv, nsa_w_cmp, w_out)
        mp = mods_all[l, :, :b].reshape(3 * N_SUB, b, 1, d)
        ms = mods_all[l, :, b:].reshape(3 * N_SUB, db, 1, d)
        hp = ffn_half(hp, mp, 0, norm_pre[l, 0], norm_post[l, 0], wg16, wu16, wd16, l, 0)
        hs = ffn_half(hs, ms, 0, norm_pre[l, 0], norm_post[l, 0], wg16, wu16, wd16, l, 0)
        hp, sp = _mixer_prompt(hp, mp, norm_pre[l, 1], norm_post[l, 1], lw, consts)
        hs, ss = _mixer_sample(hs, ms, norm_pre[l, 1], norm_post[l, 1], lw, consts, l, page_table,
                               cache_mla_ckv, cache_krt, cache_nsat, state_nsa_win, cache_fkvt, cache_lft)
        hp = ffn_half(hp, mp, 2, norm_pre[l, 2], norm_post[l, 2], wg16, wu16, wd16, l, 1)
        hs = ffn_half(hs, ms, 2, norm_pre[l, 2], norm_post[l, 2], wg16, wu16, wd16, l, 1)
        st_p.append(sp)
        st_s.append(ss)
    outs_p = tuple(jnp.stack(a) for a in zip(*st_p))
    outs_s = tuple(jnp.stack(a) for a in zip(*st_s))
    return (hp, hs) + outs_p + outs_s
```

```python
import functools

import numpy as np
import jax
import jax.numpy as jnp
from jax import lax
from jax.experimental import pallas as pl
from jax.experimental.pallas import tpu as pltpu

F32, BF16 = jnp.float32, jnp.bfloat16

H_MLA, Q_LORA, KV_LORA, NOPE_DIM, ROPE_DIM, V_DIM = 8, 384, 256, 64, 32, 64
H_NSA, NSA_DIM, L_CMP, N_SEL, WINDOW = 4, 64, 64, 16, 512
H_FOX, FOX_DIM = 4, 64
N_SUB = 3
ROPE_THETA = 10000.0
FORCED_SCORE = 1.0e4
EPS = 1e-6
MLA_SCALE = (NOPE_DIM + ROPE_DIM) ** -0.5
NSA_SCALE = NSA_DIM ** -0.5
FOX_SCALE = FOX_DIM ** -0.5
IN_SPLITS = (Q_LORA, KV_LORA, ROPE_DIM, H_NSA * NSA_DIM, 6 * NSA_DIM, H_NSA * 3,
             H_FOX * FOX_DIM, H_FOX * FOX_DIM, H_FOX * FOX_DIM, H_FOX)

LANES = 128
VMEM_LIMIT = 56 * 1024 * 1024

MISC_GATE = ROPE_DIM
MISC_LOGF = ROPE_DIM + H_NSA * 3
MISC_END = MISC_LOGF + H_FOX
FOX_XW = 6

_W_WIDTHS = (("cq", Q_LORA), ("ckv", KV_LORA), ("nq", 256), ("nqs", 256), ("nkv", 384), ("nkvs", 384),
             ("fq", 256), ("fkv", 512), ("misc", 128), ("miscs", 128), ("krp", 128), ("krps", 128))
_W_OFF = {}
_acc = 0
for _n, _w in _W_WIDTHS:
    _W_OFF[_n] = (_acc, _acc + _w)
    _acc += _w
W_TOTAL = _acc
_TAB = {n: (i * LANES, (i + 1) * LANES) for i, n in
        enumerate(("ckx", "skx", "cm", "sm", "cn", "sn", "cnkv", "snkv"))}
TAB_TOTAL = 8 * LANES


def _cparams(sem):
    return pltpu.CompilerParams(dimension_semantics=sem, vmem_limit_bytes=VMEM_LIMIT)


def _rms(x, g):
    ms = jnp.mean(x * x, axis=-1, keepdims=True)
    return x * lax.rsqrt(ms + EPS) * g


def _silu(x):
    return x / (1.0 + jnp.exp(-x))


def _softmax_rows(s):
    m = jnp.max(s, axis=-1, keepdims=True)
    m = jnp.where(m > -jnp.inf, m, 0.0)
    e = jnp.exp(s - m)
    d = jnp.sum(e, axis=-1, keepdims=True)
    return e * (1.0 / jnp.where(d > 0, d, 1.0))


def _softmax_parts(s):
    m = jnp.max(s, axis=-1, keepdims=True)
    m = jnp.where(m > -jnp.inf, m, 0.0)
    e = jnp.exp(s - m)
    d = jnp.sum(e, axis=-1, keepdims=True)
    return e, 1.0 / jnp.where(d > 0, d, 1.0)


def _dot(a, b):
    return jnp.dot(a, b, preferred_element_type=F32)


def _dot_nt(a, b):
    return lax.dot_general(a, b, (((1,), (1,)), ((), ())), preferred_element_type=F32)


def _split3(x):
    hi = x.astype(BF16)
    r1 = x - hi.astype(F32)
    mid = r1.astype(BF16)
    lo = (r1 - mid.astype(F32)).astype(BF16)
    return hi, mid, lo


def _ada_body(c_ref, w_ref, b_ref, o_ref):
    s = _silu(c_ref[...])
    o_ref[0, 0] = _dot(s.astype(BF16), w_ref[0].astype(BF16)) + b_ref[0, 0]


def ada_mods(c_all, ada_w, ada_b):
    depth, d, n9 = ada_w.shape
    nc = n9 // d
    bt = c_all.shape[0]
    return pl.pallas_call(
        _ada_body,
        grid=(depth, nc),
        in_specs=[pl.BlockSpec((bt, d), lambda l, n: (0, 0)),
                  pl.BlockSpec((1, d, d), lambda l, n: (l, 0, n)),
                  pl.BlockSpec((1, 1, 1, d), lambda l, n: (l, n, 0, 0))],
        out_specs=pl.BlockSpec((1, 1, bt, d), lambda l, n: (l, n, 0, 0)),
        out_shape=jax.ShapeDtypeStruct((depth, nc, bt, d), F32),
        compiler_params=_cparams(("arbitrary", "arbitrary")),
        name="ada_mods",
    )(c_all, ada_w, ada_b.reshape(depth, nc, 1, d))


def _ffn_body(h_ref, m_ref, gpre_ref, gpost_ref, wg_ref, wu_ref, wd_ref, o_ref, u_scr, acc_scr):
    f = pl.program_id(1)
    g_, r_, d_ = h_ref.shape

    @pl.when(f == 0)
    def _():
        y = _rms(h_ref[...], gpre_ref[...])
        u = y * (1.0 + m_ref[1]) + m_ref[0]
        u_scr[...] = u.reshape(g_ * r_, d_).astype(BF16)
        acc_scr[...] = jnp.zeros_like(acc_scr)

    u = u_scr[...]
    gate = _dot(u, wg_ref[0, 0])
    up = _dot(u, wu_ref[0, 0])
    a = (_silu(gate) * up).astype(BF16)
    acc_scr[...] += _dot(a, wd_ref[0, 0])

    @pl.when(f == pl.num_programs(1) - 1)
    def _():
        yn = _rms(acc_scr[...], gpost_ref[...]).reshape(g_, r_, d_)
        o_ref[...] = h_ref[...] + 0.5 * m_ref[2] * yn


def _tok_tile(n_rows_per_group, cap):
    return min(n_rows_per_group, cap)


def ffn_half(h, mods, sub, g_pre, g_post, wg, wu, wd, layer, half, *, tm_cap=1024, tf=256):
    b, t, d = h.shape
    dff = wg.shape[-1]
    tf = min(tf, dff)
    if t >= 8 * 16:
        tm = _tok_tile(t, tm_cap)
        blk, tpb = (1, tm, d), t // tm
        hmap = lambda i, f: (i // tpb, i % tpb, 0)
        mmap = lambda i, f: (sub, i // tpb, 0, 0)
        grid0, mblk = b * tpb, (3, 1, 1, d)
    else:
        gb = min(b, max(1, tm_cap // t))
        blk = (gb, t, d)
        hmap = lambda i, f: (i, 0, 0)
        mmap = lambda i, f: (sub, i, 0, 0)
        grid0, mblk = b // gb, (3, gb, 1, d)
    rows = blk[0] * blk[1]
    return pl.pallas_call(
        _ffn_body,
        grid=(grid0, dff // tf),
        in_specs=[pl.BlockSpec(blk, hmap),
                  pl.BlockSpec(mblk, mmap),
                  pl.BlockSpec((1, d), lambda i, f: (0, 0)),
                  pl.BlockSpec((1, d), lambda i, f: (0, 0)),
                  pl.BlockSpec((1, 1, d, tf), lambda i, f: (layer, half, 0, f)),
                  pl.BlockSpec((1, 1, d, tf), lambda i, f: (layer, half, 0, f)),
                  pl.BlockSpec((1, 1, tf, d), lambda i, f: (layer, half, f, 0))],
        out_specs=pl.BlockSpec(blk, hmap),
        out_shape=jax.ShapeDtypeStruct(h.shape, F32),
        scratch_shapes=[pltpu.VMEM((rows, d), BF16), pltpu.VMEM((rows, d), F32)],
        compiler_params=_cparams(("arbitrary", "arbitrary")),
        name="ffn_half",
    )(h, mods, g_pre.reshape(1, d), g_post.reshape(1, d), wg, wu, wd)


def _swap_halves(w, dh):
    k = w.shape[0]
    w4 = w.reshape(k, w.shape[1] // dh, 2, dh // 2)
    return w4[:, :, ::-1, :].reshape(k, -1)


def _inproj_weights(w_in, w_uq, w_uk, w_uv):
    k = w_in.shape[0]
    cuts = np.cumsum(IN_SPLITS)[:-1].tolist()
    cq, ckv, kr, nq, nkv, ng, fq, fk, fv, ff = jnp.split(w_in, cuts, axis=1)
    z = lambda n: jnp.zeros((k, n), w_in.dtype)
    nkv3 = nkv.reshape(k, 3, 2, NSA_DIM)
    nkvs = jnp.stack([_swap_halves(nkv3[:, :, 0].reshape(k, -1), NSA_DIM).reshape(k, 3, NSA_DIM),
                      jnp.zeros((k, 3, NSA_DIM), w_in.dtype)], axis=2).reshape(k, -1)
    krs = _swap_halves(kr, ROPE_DIM)
    cols = dict(cq=cq, ckv=ckv, nq=nq, nqs=_swap_halves(nq, NSA_DIM), nkv=nkv, nkvs=nkvs, fq=fq,
                fkv=jnp.concatenate([fk, fv], axis=1),
                misc=jnp.concatenate([kr, ng, ff, z(LANES - MISC_END)], axis=1),
                miscs=jnp.concatenate([krs, z(LANES - ROPE_DIM)], axis=1),
                krp=jnp.concatenate([kr, kr, z(LANES - 2 * ROPE_DIM)], axis=1),
                krps=jnp.concatenate([krs, krs, z(LANES - 2 * ROPE_DIM)], axis=1))
    w_all = jnp.concatenate([cols[n] for n, _ in _W_WIDTHS], axis=1).astype(BF16)
    dq = NOPE_DIM + ROPE_DIM
    uq = w_uq.reshape(Q_LORA, H_MLA, dq)
    wqn = uq[:, :, :NOPE_DIM].reshape(Q_LORA, -1)
    rope = uq[:, :, NOPE_DIM:]
    zr = jnp.zeros((Q_LORA, H_MLA // 2, LANES - 2 * ROPE_DIM), w_uq.dtype)

    def pairs(r):
        return jnp.concatenate([r.reshape(Q_LORA, H_MLA // 2, 2 * ROPE_DIM), zr], axis=2).reshape(Q_LORA, -1)

    wqr = pairs(rope)
    wqs = pairs(_swap_halves(rope.reshape(Q_LORA, -1), ROPE_DIM).reshape(Q_LORA, H_MLA, ROPE_DIM))
    wuk = w_uk.reshape(KV_LORA, -1)
    wuv = w_uv.reshape(KV_LORA, -1)
    ukt = jnp.transpose(w_uk, (1, 2, 0))
    zt = jnp.zeros_like(ukt)
    even = jnp.concatenate([ukt, zt], axis=1)
    odd = jnp.concatenate([zt, ukt], axis=1)
    wukt = jnp.where((jnp.arange(H_MLA) % 2 == 0)[:, None, None], even, odd)
    return dict(w_all=w_all, wqn=wqn.astype(BF16), wqr=wqr.astype(BF16), wqs=wqs.astype(BF16),
                wuk=wuk.astype(BF16), wuv=wuv.astype(BF16), wukt=wukt.astype(BF16))


def _rope_tables(pos):
    def cs(dh):
        half = dh // 2
        inv = jnp.power(ROPE_THETA, -jnp.arange(half, dtype=F32) / half)
        ang = pos.astype(F32)[:, None] * inv[None, :]
        c, s = jnp.cos(ang), jnp.sin(ang)
        return jnp.concatenate([c, c], 1), jnp.concatenate([-s, s], 1)

    n = pos.shape[0]
    c32, s32 = cs(ROPE_DIM)
    c64, s64 = cs(NSA_DIM)
    one, zero = jnp.ones((n, 1), F32), jnp.zeros((n, 1), F32)
    rep = lambda a, k: jnp.tile(a, (1, k))
    tabs = dict(
        ckx=jnp.concatenate([c32, c32, rep(zero, 64)], 1), skx=jnp.concatenate([s32, s32, rep(zero, 64)], 1),
        cm=jnp.concatenate([c32, rep(one, 96)], 1), sm=jnp.concatenate([s32, rep(zero, 96)], 1),
        cn=NSA_SCALE * jnp.concatenate([c64, c64], 1), sn=NSA_SCALE * jnp.concatenate([s64, s64], 1),
        cnkv=jnp.concatenate([c64, rep(one, 64)], 1), snkv=jnp.concatenate([s64, rep(zero, 64)], 1))
    return jnp.concatenate([tabs[k] for k in _TAB], axis=1)


def _fox_place():
    pq = np.zeros((3 * LANES, 2 * LANES), np.float32)
    pk = np.zeros((3 * LANES, 2 * LANES), np.float32)
    cq = np.zeros((1, 2 * LANES), np.float32)
    ck = np.zeros((1, 2 * LANES), np.float32)
    for h in range(H_FOX):
        base = (h // 2) * LANES + (h % 2) * FOX_XW
        for s in range(3):
            pq[s * LANES + MISC_LOGF + h, base + s] = 1.0
            pk[s * LANES + MISC_LOGF + h, base + 3 + s] = -1.0
            cq[0, base + 3 + s] = 1.0
            ck[0, base + s] = 1.0
    return (jnp.asarray(pq, BF16), jnp.asarray(pk, BF16), jnp.asarray(cq), jnp.asarray(ck))


def _inproj_body(decode, tpb, *refs):
    (h_ref, m_ref, gpre_ref, w_ref, gq_ref, gkv_ref, wqn_ref, wqr_ref, wqs_ref, wuk_ref, wuv_ref, wukt_ref,
     tab_ref, bvec_ref, ltri_ref, pq_ref, pk_ref, cq_ref, ck_ref) = refs[:19]
    outs = refs[19:-1]
    carry_scr = refs[-1]
    (ckv_o, nsa_o, win_o, fox_o, misc_o, qn_o, qx_o, kx_o, nq_o, fq_o, fkv16_o) = outs[:11]
    g_, r_, d_ = h_ref.shape
    tm = g_ * r_
    y = _rms(h_ref[...], gpre_ref[...])
    u = (y * (1.0 + m_ref[1]) + m_ref[0]).reshape(tm, d_).astype(BF16)

    def z(name):
        a, b = _W_OFF[name]
        return _dot(u, w_ref[:, a:b])

    def tab(name, k=1):
        a, b = _TAB[name]
        t = tab_ref[:, a:b]
        return t if k == 1 else jnp.concatenate([t] * k, axis=1)

    cqn = _rms(z("cq"), gq_ref[...]).astype(BF16)
    qn = _dot(cqn, wqn_ref[...]) * MLA_SCALE
    qn16 = qn.astype(BF16)
    qn_o[...] = qn16.astype(qn_o.dtype)
    qx = (_dot(cqn, wqr_ref[...]) * tab("ckx", 4) + _dot(cqn, wqs_ref[...]) * tab("skx", 4)) * MLA_SCALE
    qx_o[...] = qx.astype(qx_o.dtype)
    ckvn = _rms(z("ckv"), gkv_ref[...])
    ckv_o[...] = ckvn
    kx_o[...] = (z("krp") * tab("ckx") + z("krps") * tab("skx")).astype(kx_o.dtype)
    if decode:
        qlat_o, fcum_o = outs[11:]
        for h in range(H_MLA):
            p = h // 2
            qlat_o[:, h * KV_LORA:(h + 1) * KV_LORA] = _dot(qn16[:, p * LANES:(p + 1) * LANES], wukt_ref[h])
    else:
        kn_o, vm_o, fqx_o, fkx_o = outs[11:]
        ck16 = ckvn.astype(BF16)
        kn_o[...] = _dot(ck16, wuk_ref[...]).astype(BF16)
        vm_o[...] = _dot(ck16, wuv_ref[...]).astype(BF16)
    nq_o[...] = (z("nq") * tab("cn", 2) + z("nqs") * tab("sn", 2)).astype(nq_o.dtype)
    nkv = z("nkv") * tab("cnkv", 3) + z("nkvs") * tab("snkv", 3)
    nsa_o[...] = nkv[:, :4 * NSA_DIM]
    win_o[...] = nkv[:, 4 * NSA_DIM:]
    fq_o[...] = (z("fq") * FOX_SCALE).astype(fq_o.dtype)
    fkv = z("fkv")
    fox_o[...] = fkv
    fkv16_o[...] = fkv.astype(fkv16_o.dtype)
    zm = z("misc")
    lane = lax.broadcasted_iota(jnp.int32, (tm, LANES), 1)
    roped = zm * tab("cm") + z("miscs") * tab("sm")
    xb = zm + bvec_ref[...]
    logsig = jnp.minimum(xb, 0.0) - jnp.log(1.0 + jnp.exp(-jnp.abs(xb)))
    misc = jnp.where(lane < MISC_GATE, roped,
                     jnp.where(lane < MISC_LOGF, 1.0 / (1.0 + jnp.exp(-zm)),
                               jnp.where(lane < MISC_END, logsig, 0.0)))
    misc_o[...] = misc
    lf = jnp.where((lane >= MISC_LOGF) & (lane < MISC_END), misc, 0.0)
    ltri = ltri_ref[...]
    hi, mid, lo = _split3(lf)
    cum = _dot(ltri, hi) + _dot(ltri, mid) + _dot(ltri, lo)
    first = (pl.program_id(0) % tpb) == 0
    carry = jnp.where(first, 0.0, carry_scr[0:1, :])
    fc = cum + carry
    carry_scr[0:1, :] = fc[tm - 1:tm, :]
    if decode:
        fcum_o[...] = fc
    else:
        g3 = jnp.concatenate(list(_split3(fc)), axis=1)
        fqx_o[...] = (_dot(g3, pq_ref[...]) + cq_ref[...]).astype(BF16)
        fkx_o[...] = (_dot(g3, pk_ref[...]) + ck_ref[...]).astype(BF16)


def in_proj(h, mods, g_pre, lw, tables, bvec, ltri, place, *, decode, tm_cap=512):
    b, t, d = h.shape
    n = b * t
    if not decode:
        tm = _tok_tile(t, tm_cap)
        tpb = t // tm
        blk = (1, tm, d)
        hmap = lambda i: (i // tpb, i % tpb, 0)
        mmap = lambda i: (1, i // tpb, 0, 0)
        tmap = lambda i: (i % tpb, 0)
        grid0, mblk = b * tpb, (3, 1, 1, d)
    else:
        gb = min(b, max(1, tm_cap // t))
        tm, tpb = gb * t, 1
        blk = (gb, t, d)
        hmap = lambda i: (i, 0, 0)
        mmap = lambda i: (1, i, 0, 0)
        tmap = lambda i: (0, 0)
        grid0, mblk = b // gb, (3, gb, 1, d)
    full = lambda a: pl.BlockSpec(a.shape, lambda i: (0,) * a.ndim)
    pq, pk, cq, ck = place
    ins = [h, mods, g_pre.reshape(1, d), lw["w_all"], lw["g_q"].reshape(1, -1), lw["g_kv"].reshape(1, -1),
           lw["wqn"], lw["wqr"], lw["wqs"], lw["wuk"], lw["wuv"], lw["wukt"], tables, bvec, ltri, pq, pk, cq, ck]
    in_specs = [pl.BlockSpec(blk, hmap), pl.BlockSpec(mblk, mmap)] + [full(a) for a in ins[2:12]] + \
               [pl.BlockSpec((tm, TAB_TOTAL), tmap)] + [full(a) for a in ins[13:]]
    names = ["ckv", "nsa", "win", "fox", "misc", "qn", "qx", "kx", "nq", "fq", "fkv16"]
    widths = [KV_LORA, 256, 128, 512, 128, 512, 512, 128, 256, 256, 512]
    dts = [F32] * 5 + [F32 if decode else BF16] * 6
    if decode:
        names += ["qlat", "fcum"]
        widths += [H_MLA * KV_LORA, 128]
        dts += [F32, F32]
    else:
        names += ["kn", "vm", "fqx", "fkx"]
        widths += [512, 512, 256, 256]
        dts += [BF16, BF16, BF16, BF16]
    outs = pl.pallas_call(
        functools.partial(_inproj_body, decode, tpb),
        grid=(grid0,),
        in_specs=in_specs,
        out_specs=[pl.BlockSpec((tm, w), lambda i: (i, 0)) for w in widths],
        out_shape=[jax.ShapeDtypeStruct((n, w), dt) for w, dt in zip(widths, dts)],
        scratch_shapes=[pltpu.VMEM((8, LANES), F32)],
        compiler_params=_cparams(("arbitrary",)),
        name="in_proj_dec" if decode else "in_proj",
    )(*ins)
    return dict(zip(names, outs))


def _flash_body(xw, tq, tk, nsplit, qa_ref, qx_ref, ka_ref, kx_ref, v_ref, o_ref, m_scr, l_scr, acc_scr):
    i = pl.program_id(2)
    lane = lax.broadcasted_iota(jnp.int32, (tq, LANES), 1)
    qa, qx = qa_ref[0], qx_ref[0]
    zero = jnp.zeros_like(qa)
    q0 = jnp.concatenate([jnp.where(lane < 64, qa, zero), jnp.where(lane < xw, qx, zero)], axis=1)
    q1 = jnp.concatenate([jnp.where(lane >= 64, qa, zero),
                          jnp.where((lane >= xw) & (lane < 2 * xw), qx, zero)], axis=1)
    q2 = jnp.concatenate([q0, q1], axis=0)
    m_scr[...] = jnp.full_like(m_scr, -jnp.inf)
    l_scr[...] = jnp.zeros_like(l_scr)
    acc_scr[...] = jnp.zeros_like(acc_scr)

    def chunk(c, masked):
        k0 = pl.multiple_of(c * tk, tk)
        k2 = jnp.concatenate([ka_ref[0, pl.ds(k0, tk), :], kx_ref[0, pl.ds(k0, tk), :]], axis=1)
        v = v_ref[0, pl.ds(k0, tk), :]
        tr = tq // nsplit
        if masked:
            row = lax.broadcasted_iota(jnp.int32, (tr, tk), 0)
            col = lax.broadcasted_iota(jnp.int32, (tr, tk), 1)
        for ch in range(2 * nsplit):
            rs = slice(ch * tr, (ch + 1) * tr)
            s = _dot_nt(q2[rs], k2)
            if masked:
                s = jnp.where(k0 + col <= i * tq + (ch % nsplit) * tr + row, s, -jnp.inf)
            m_prev = m_scr[rs]
            m_new = jnp.maximum(m_prev, jnp.max(s, axis=1, keepdims=True))
            alpha = jnp.exp(m_prev - m_new)
            p = jnp.exp(s - m_new[:, :1])
            l_scr[rs] = alpha * l_scr[rs] + jnp.sum(p, axis=1, keepdims=True)
            acc_scr[rs] = alpha * acc_scr[rs] + _dot(p.astype(BF16), v)
            m_scr[rs] = m_new

    n_full = (i * tq) // tk

    def body(c, carry):
        chunk(c, False)
        return carry

    lax.fori_loop(0, n_full, body, 0)
    chunk(n_full, True)
    o = acc_scr[...] * (1.0 / l_scr[...])
    o_ref[0] = jnp.where(lane < 64, o[:tq], o[tq:]).astype(o_ref.dtype)


def flash_pairs(qa, qx, ka, kx, v, *, xw, kx_shared, v_off=0, tq=512, tk=512):
    b, t, w = qa.shape
    npair = w // LANES
    tq, tk = min(tq, t), min(tk, t)
    kxmap = (lambda bb, p, i: (bb, 0, 0)) if kx_shared else (lambda bb, p, i: (bb, 0, p))
    return pl.pallas_call(
        functools.partial(_flash_body, xw, tq, tk, 1),
        grid=(b, npair, t // tq),
        in_specs=[pl.BlockSpec((1, tq, LANES), lambda bb, p, i: (bb, i, p)),
                  pl.BlockSpec((1, tq, LANES), lambda bb, p, i: (bb, i, p)),
                  pl.BlockSpec((1, t, LANES), lambda bb, p, i: (bb, 0, p)),
                  pl.BlockSpec((1, t, LANES), kxmap),
                  pl.BlockSpec((1, t, LANES), lambda bb, p, i: (bb, 0, v_off + p))],
        out_specs=pl.BlockSpec((1, tq, LANES), lambda bb, p, i: (bb, i, p)),
        out_shape=jax.ShapeDtypeStruct((b, t, w), BF16),
        scratch_shapes=[pltpu.VMEM((2 * tq, LANES), F32)] * 3,
        compiler_params=_cparams(("arbitrary", "arbitrary", "arbitrary")),
        name="flash_pairs",
    )(qa, qx, ka, kx, v)


def _compress_body(x_ref, w_ref, o_ref):
    @pl.when(pl.program_id(0) == 0)
    def _():
        o_ref[...] = jnp.zeros_like(o_ref)

    o_ref[...] += _dot(x_ref[...].astype(BF16), w_ref[...])


def _cmp_weight(w_cmp):
    z = jnp.zeros((L_CMP, NSA_DIM, NSA_DIM), w_cmp.dtype)
    top = jnp.concatenate([w_cmp[0], z], axis=2)
    bot = jnp.concatenate([z, w_cmp[1]], axis=2)
    zz = jnp.zeros((L_CMP, 2 * NSA_DIM, 2 * NSA_DIM), w_cmp.dtype)
    return jnp.concatenate([top, bot, zz], axis=1).reshape(L_CMP * 4 * NSA_DIM, 2 * NSA_DIM).astype(BF16)


def compress_blocks(nsa_rows, wbig, *, tk=2048):
    nblk, kk = nsa_rows.shape
    tk = min(tk, kk)
    return pl.pallas_call(
        _compress_body,
        grid=(kk // tk,),
        in_specs=[pl.BlockSpec((nblk, tk), lambda k: (0, k)), pl.BlockSpec((tk, LANES), lambda k: (k, 0))],
        out_specs=pl.BlockSpec((nblk, LANES), lambda k: (0, 0)),
        out_shape=jax.ShapeDtypeStruct((nblk, LANES), F32),
        compiler_params=_cparams(("arbitrary",)),
        name="compress_blocks",
    )(nsa_rows, wbig)


def _head_slabs(qf, lane):
    slabs = []
    for p in range(H_NSA // 2):
        x = qf[:, p * LANES:(p + 1) * LANES]
        slabs.append(jnp.where(lane < 64, x, 0.0))
        slabs.append(jnp.where(lane < 64, pltpu.roll(x, 64, 1), 0.0))
    return jnp.concatenate(slabs, axis=0)


def _select_blocks(imp, validf, bid, lane_bids, n_sel):
    rank = jnp.zeros(imp.shape, F32)
    for j, bj in lane_bids:
        col = imp[:, j:j + 1]
        tie = jnp.where(bid > bj, 1.0, 0.0)
        rank = rank + jnp.where(col > imp, 1.0, jnp.where(col == imp, tie, 0.0))
    return jnp.where(rank < n_sel, validf, 0.0)


def _block_importance(imp, bid, cur, nb):
    imp = jnp.where(bid == 0, FORCED_SCORE, jnp.where(bid == cur, FORCED_SCORE,
                                                      jnp.where(bid == cur - 1, FORCED_SCORE, imp)))
    validf = jnp.where(bid <= cur, jnp.where(bid < nb, 1.0, 0.0), 0.0)
    return jnp.where(validf > 0.5, imp, -jnp.inf), validf


def _nsa_prompt_body(tq, t, nb, span, n_sel, kstep, *refs):
    i = pl.program_id(1)
    need = ((i + 1) * tq + kstep - 1) // kstep
    for v in range(1, t // kstep + 1):
        @pl.when(need == v)
        def _():
            _nsa_prompt_tile(v * kstep, tq, t, nb, span, n_sel, i, *refs)


def _nsa_prompt_tile(klen, tq, t, nb, span, n_sel, i, q_ref, ckv_ref, kv_ref, win_ref, misc_ref, e_ref, o_ref):
    lane = lax.broadcasted_iota(jnp.int32, (tq, LANES), 1)
    q = _head_slabs(q_ref[0].astype(F32), lane).astype(BF16)
    rows = H_NSA * tq
    row4 = lax.broadcasted_iota(jnp.int32, (rows, LANES), 0)
    qpos4 = i * tq + (row4 & (tq - 1))
    bi4 = lax.broadcasted_iota(jnp.int32, (rows, LANES), 1)
    ckv16 = ckv_ref[0].astype(BF16)
    sc = jnp.where((bi4 + 1) * L_CMP - 1 <= qpos4, jnp.where(bi4 < nb, _dot_nt(q, ckv16), -jnp.inf), -jnp.inf)
    pc = _softmax_rows(sc)
    oc = _dot(pc.astype(BF16), ckv16)
    imp = pc[0:tq] + pc[tq:2 * tq] + pc[2 * tq:3 * tq] + pc[3 * tq:4 * tq]
    qpos1 = i * tq + lax.broadcasted_iota(jnp.int32, (tq, LANES), 0)
    imp, validf = _block_importance(imp, lane, qpos1 // L_CMP, nb)
    sel = _select_blocks(imp, validf, lane, tuple((j, j) for j in range(nb)), n_sel)
    seltok = _dot(sel.astype(BF16), e_ref[:, :klen])
    kpos = lax.broadcasted_iota(jnp.int32, (tq, klen), 1)
    qpos_t = i * tq + lax.broadcasted_iota(jnp.int32, (tq, klen), 0)
    bias = jnp.where(kpos <= qpos_t, jnp.where(seltok > 0.5, 0.0, -jnp.inf), -jnp.inf)
    kv16 = kv_ref[0, :klen, :].astype(BF16)
    es, rs = _softmax_parts(_dot_nt(q, kv16) + jnp.concatenate([bias] * H_NSA, axis=0))
    osel = _dot(es.astype(BF16), kv16) * rs
    start = jnp.clip(i * tq - WINDOW, 0, t - span)
    start = pl.multiple_of(start, 8)
    w16 = win_ref[0, pl.ds(start, span), :].astype(BF16)
    roww = lax.broadcasted_iota(jnp.int32, (rows, span), 0)
    diff = i * tq + (roww & (tq - 1)) - (start + lax.broadcasted_iota(jnp.int32, (rows, span), 1))
    ew, rw = _softmax_parts(jnp.where(diff >= 0, jnp.where(diff <= WINDOW, _dot_nt(q, w16), -jnp.inf), -jnp.inf))
    ow = _dot(ew.astype(BF16), w16) * rw
    g = misc_ref[0]
    heads = []
    for h in range(H_NSA):
        sl = slice(h * tq, (h + 1) * tq)
        c0 = MISC_GATE + 3 * h
        heads.append(g[:, c0:c0 + 1] * oc[sl] + g[:, c0 + 1:c0 + 2] * osel[sl] + g[:, c0 + 2:c0 + 3] * ow[sl])
    pairs = [jnp.where(lane < 64, pltpu.roll(heads[2 * p], 64, 1), heads[2 * p + 1]) for p in range(H_NSA // 2)]
    o_ref[0] = jnp.concatenate(pairs, axis=1).astype(o_ref.dtype)


def nsa_prompt(nq, cmpkv, nsa, win, misc, *, tq=128):
    b, t, _ = nq.shape
    nb = -(-t // L_CMP)
    tq = min(tq, t)
    span = min(t, WINDOW + tq)
    n_sel = min(N_SEL, nb)
    e = (jnp.arange(t)[None, :] // L_CMP == jnp.arange(LANES)[:, None]).astype(BF16)
    kstep = min(t, max(tq, 512))
    return pl.pallas_call(
        functools.partial(_nsa_prompt_body, tq, t, nb, span, n_sel, kstep),
        grid=(b, t // tq),
        in_specs=[pl.BlockSpec((1, tq, 256), lambda bb, i: (bb, i, 0)),
                  pl.BlockSpec((1, LANES, LANES), lambda bb, i: (bb, 0, 0)),
                  pl.BlockSpec((1, t, LANES), lambda bb, i: (bb, 0, 1)),
                  pl.BlockSpec((1, t, LANES), lambda bb, i: (bb, 0, 0)),
                  pl.BlockSpec((1, tq, LANES), lambda bb, i: (bb, i, 0)),
                  pl.BlockSpec((LANES, t), lambda bb, i: (0, 0))],
        out_specs=pl.BlockSpec((1, tq, 256), lambda bb, i: (bb, i, 0)),
        out_shape=jax.ShapeDtypeStruct((b, t, 256), BF16),
        compiler_params=_cparams(("arbitrary", "arbitrary")),
        name="nsa_prompt",
    )(nq, cmpkv, nsa, win, misc, e)


def _outproj_body(decode, *refs):
    if decode:
        ol_ref, wuvb_ref, on_ref, of_ref, w_ref, h_ref, m_ref, gpost_ref, o_ref = refs
        om = _dot(ol_ref[...].astype(BF16), wuvb_ref[...]).astype(BF16)
    else:
        om_ref, on_ref, of_ref, w_ref, h_ref, m_ref, gpost_ref, o_ref = refs
        om = om_ref[...]
    g_, r_, d_ = h_ref.shape
    o = jnp.concatenate([om, on_ref[...].astype(BF16), of_ref[...].astype(BF16)], axis=1)
    yn = _rms(_dot(o, w_ref[...]), gpost_ref[...]).reshape(g_, r_, d_)
    o_ref[...] = h_ref[...] + m_ref[2] * yn


def out_proj(o_mla, o_nsa, o_fox, w_out, h, mods, g_post, *, wuv_bd=None, tm_cap=512):
    b, t, d = h.shape
    decode = wuv_bd is not None
    if not decode:
        tm = _tok_tile(t, tm_cap)
        tpb = t // tm
        blk = (1, tm, d)
        hmap = lambda i: (i // tpb, i % tpb, 0)
        mmap = lambda i: (1, i // tpb, 0, 0)
        grid0, mblk = b * tpb, (3, 1, 1, d)
    else:
        gb = min(b, max(1, tm_cap // t))
        tm = gb * t
        blk = (gb, t, d)
        hmap = lambda i: (i, 0, 0)
        mmap = lambda i: (1, i, 0, 0)
        grid0, mblk = b // gb, (3, gb, 1, d)
    row = lambda a: pl.BlockSpec((tm, a.shape[1]), lambda i: (i, 0))
    full = lambda a: pl.BlockSpec(a.shape, lambda i: (0,) * a.ndim)
    ins = [o_mla] + ([wuv_bd] if decode else []) + [o_nsa, o_fox, w_out, h, mods, g_post.reshape(1, d)]
    specs = [row(o_mla)] + ([full(wuv_bd)] if decode else []) + \
            [row(o_nsa), row(o_fox), full(w_out), pl.BlockSpec(blk, hmap), pl.BlockSpec(mblk, mmap),
             pl.BlockSpec((1, d), lambda i: (0, 0))]
    return pl.pallas_call(
        functools.partial(_outproj_body, decode),
        grid=(grid0,),
        in_specs=specs,
        out_specs=pl.BlockSpec(blk, hmap),
        out_shape=jax.ShapeDtypeStruct(h.shape, F32),
        compiler_params=_cparams(("arbitrary",)),
        name="out_proj_dec" if decode else "out_proj",
    )(*ins)


NEW_PAD = LANES


def _page_copy(spec, layer, page, slot, j):
    cache_ref, buf, sem, src_rows, mode, width = spec
    src = cache_ref.at[layer, page]
    if src_rows is not None:
        src = src.at[pl.ds(src_rows[0], src_rows[1])]
    if mode == "lanes":
        dst = buf.at[slot, :, pl.ds(j * width, width)]
    elif mode == "mid":
        dst = buf.at[slot, :, j, :]
    else:
        dst = buf.at[slot, pl.ds(j * width, width)]
    return pltpu.make_async_copy(src, dst, sem.at[slot])


def _fetch_pages(pt_ref, bb, slot, layer, n_pages, specs):
    def body(j, c):
        page = pt_ref[bb, j]
        for spec in specs:
            _page_copy(spec, layer, page, slot, j).start()
        return c

    lax.fori_loop(0, n_pages, body, 0, unroll=min(8, n_pages))


def _wait_pages(slot, layer, n_pages, specs):
    for j in range(n_pages):
        for spec in specs:
            _page_copy(spec, layer, 0, slot, j).wait()


def _gather_step(pt_ref, layer, n_pages, specs):
    b = pl.program_id(0)
    slot = b % 2

    @pl.when(b == 0)
    def _():
        _fetch_pages(pt_ref, 0, 0, layer, n_pages, specs)

    @pl.when(b + 1 < pl.num_programs(0))
    def _():
        _fetch_pages(pt_ref, b + 1, 1 - slot, layer, n_pages, specs)

    _wait_pages(slot, layer, n_pages, specs)
    return slot


def _pad_new(x):
    dt, w = x.shape
    return jnp.concatenate([x, jnp.zeros((NEW_PAD - dt, w), x.dtype)], axis=0).astype(BF16)


def _new_bias(rows, dt, ok=None):
    r = lax.broadcasted_iota(jnp.int32, (rows, NEW_PAD), 0)
    c = lax.broadcasted_iota(jnp.int32, (rows, NEW_PAD), 1)
    vis = c <= (r % dt)
    if ok is not None:
        vis = vis & ok
    return jnp.where(vis, 0.0, -jnp.inf)


def _joint_softmax_pv(sa, sb, va, vb, va_t=False):
    m = jnp.maximum(jnp.max(sa, axis=-1, keepdims=True), jnp.max(sb, axis=-1, keepdims=True))
    m = jnp.where(m > -jnp.inf, m, 0.0)
    ea, eb = jnp.exp(sa - m), jnp.exp(sb - m)
    d = jnp.sum(ea, axis=-1, keepdims=True) + jnp.sum(eb, axis=-1, keepdims=True)
    oa = _dot_nt(ea.astype(BF16), va) if va_t else _dot(ea.astype(BF16), va)
    o = oa + _dot(eb.astype(BF16), vb)
    return o * (1.0 / jnp.where(d > 0, d, 1.0))


def _mla_dec_body(layer, n_pages, page, pt_ref, qlat_ref, qx_ref, ckvn_ref, misc_ref, ckv_hbm, kr_hbm, o_ref,
                  ckv_buf, kr_buf, sem_c, sem_r):
    dt = qlat_ref.shape[1]
    slot = _gather_step(pt_ref, layer, n_pages, [(ckv_hbm, ckv_buf, sem_c, None, "rows", page),
                                                 (kr_hbm, kr_buf, sem_r, None, "lanes", page)])
    ql = qlat_ref[0]
    q = jnp.concatenate([ql[:, h * KV_LORA:(h + 1) * KV_LORA] for h in range(H_MLA)], axis=0).astype(BF16)
    lane = lax.broadcasted_iota(jnp.int32, (dt, LANES), 1)
    qxf = qx_ref[0]
    parts = []
    for h in range(H_MLA):
        x = qxf[:, (h // 2) * LANES:(h // 2 + 1) * LANES]
        if h % 2:
            x = pltpu.roll(x, LANES - ROPE_DIM, 1)
        parts.append(jnp.where(lane < ROPE_DIM, x, 0.0))
    qr = jnp.concatenate(parts, axis=0)[:, :ROPE_DIM].astype(BF16)
    ckv16 = ckv_buf[slot].astype(BF16)
    krt16 = kr_buf[slot].astype(BF16)
    ckvn16 = _pad_new(ckvn_ref[0])
    krn16 = _pad_new(misc_ref[0][:, :ROPE_DIM])
    sa = _dot_nt(q, ckv16) + _dot(qr, krt16)
    sb = _dot_nt(q, ckvn16) + _dot_nt(qr, krn16) + _new_bias(H_MLA * dt, dt)
    o = _joint_softmax_pv(sa, sb, ckv16, ckvn16)
    for h in range(H_MLA):
        o_ref[0, :, h * KV_LORA:(h + 1) * KV_LORA] = o[h * dt:(h + 1) * dt].astype(o_ref.dtype)


def mla_decode(layer, page_table, qlat, qx, ckv_new, misc, cache_ckv, cache_krt):
    db, dt, _ = qlat.shape
    n_pages, page = page_table.shape[1], cache_ckv.shape[2]
    past = n_pages * page
    blk = lambda w: pl.BlockSpec((1, dt, w), lambda b, pt: (b, 0, 0))
    any_spec = pl.BlockSpec(memory_space=pl.ANY)
    return pl.pallas_call(
        functools.partial(_mla_dec_body, layer, n_pages, page),
        grid_spec=pltpu.PrefetchScalarGridSpec(
            num_scalar_prefetch=1, grid=(db,),
            in_specs=[blk(H_MLA * KV_LORA), blk(qx.shape[-1]), blk(KV_LORA), blk(LANES), any_spec, any_spec],
            out_specs=blk(H_MLA * KV_LORA),
            scratch_shapes=[pltpu.VMEM((2, past, KV_LORA), F32), pltpu.VMEM((2, ROPE_DIM, past), F32),
                            pltpu.SemaphoreType.DMA((2,)), pltpu.SemaphoreType.DMA((2,))]),
        out_shape=jax.ShapeDtypeStruct((db, dt, H_MLA * KV_LORA), F32),
        compiler_params=_cparams(("arbitrary",)),
        name="mla_decode",
    )(page_table, qlat, qx, ckv_new, misc, cache_ckv, cache_krt)


def _fox_dec_body(layer, n_pages, page, pt_ref, fq_ref, fnew_ref, fcum_ref, kv_hbm, lf_hbm, o_ref,
                  kt_buf, vt_buf, lf_buf, sem_k, sem_v, sem_l):
    dt = fq_ref.shape[1]
    rows = H_FOX * dt
    hw = H_FOX * FOX_DIM
    slot = _gather_step(pt_ref, layer, n_pages, [(kv_hbm, kt_buf, sem_k, (0, hw), "lanes", page),
                                                 (kv_hbm, vt_buf, sem_v, (hw, hw), "lanes", page),
                                                 (lf_hbm, lf_buf, sem_l, None, "rows", H_FOX)])
    lane = lax.broadcasted_iota(jnp.int32, (dt, hw), 1)
    qf = fq_ref[0]
    q = jnp.concatenate([jnp.where(lane // FOX_DIM == h, qf, 0.0) for h in range(H_FOX)], axis=0).astype(BF16)
    kt16 = kt_buf[slot].astype(BF16)
    vt16 = vt_buf[slot].astype(BF16)
    fnew = fnew_ref[0]
    kn16, vn16 = _pad_new(fnew[:, :hw]), _pad_new(fnew[:, hw:])
    nr = n_pages * H_FOX
    lf = lf_buf[slot]
    ri = lax.broadcasted_iota(jnp.int32, (page, page), 0)
    ci = lax.broadcasted_iota(jnp.int32, (page, page), 1)
    tri = jnp.where(ri > ci, 1.0, 0.0).astype(BF16)
    ones = jnp.ones((page, page), BF16)
    pr = lax.broadcasted_iota(jnp.int32, (nr, nr), 0)
    pc = lax.broadcasted_iota(jnp.int32, (nr, nr), 1)
    upper = jnp.where((pc % H_FOX == pr % H_FOX) & (pc // H_FOX > pr // H_FOX), 1.0, 0.0).astype(BF16)
    l3 = _split3(lf)
    within = _dot(l3[0], tri) + _dot(l3[1], tri) + _dot(l3[2], tri)
    tot = _dot(l3[0], ones) + _dot(l3[1], ones) + _dot(l3[2], ones)
    t3 = _split3(tot)
    r2 = within + _dot(upper, t3[0]) + _dot(upper, t3[1]) + _dot(upper, t3[2])
    bias_rows = []
    for h in range(H_FOX):
        flat = jnp.concatenate([r2[j * H_FOX + h:j * H_FOX + h + 1, :] for j in range(n_pages)], axis=1)
        bias_rows.append(jnp.broadcast_to(flat, (dt, flat.shape[1])))
    fc = fcum_ref[0]
    gt_col = jnp.concatenate([fc[:, MISC_LOGF + h:MISC_LOGF + h + 1] for h in range(H_FOX)], axis=0)
    sa = _dot(q, kt16) + jnp.concatenate(bias_rows, axis=0) + gt_col
    lane_r = lax.broadcasted_iota(jnp.int32, (rows, LANES), 1)
    row_r = lax.broadcasted_iota(jnp.int32, (rows, LANES), 0)
    onehot = jnp.where(lane_r == MISC_LOGF + row_r // dt, 1.0, 0.0).astype(BF16)
    f3 = _split3(jnp.concatenate([fc, jnp.zeros((NEW_PAD - dt, LANES), F32)], axis=0))
    gt_row = _dot_nt(onehot, f3[0]) + _dot_nt(onehot, f3[1]) + _dot_nt(onehot, f3[2])
    sb = _dot_nt(q, kn16) + gt_col - gt_row + _new_bias(rows, dt)
    o = _joint_softmax_pv(sa, sb, vt16, vn16, va_t=True)
    out = jnp.zeros((dt, hw), F32)
    for h in range(H_FOX):
        out = out + jnp.where(lane // FOX_DIM == h, o[h * dt:(h + 1) * dt], 0.0)
    o_ref[0] = out.astype(o_ref.dtype)


def fox_decode(layer, page_table, fq, fox_new, fcum, cache_kvt, cache_lft):
    db, dt, hw = fq.shape
    n_pages, page = page_table.shape[1], cache_kvt.shape[3]
    past = n_pages * page
    blk = lambda w: pl.BlockSpec((1, dt, w), lambda b, pt: (b, 0, 0))
    any_spec = pl.BlockSpec(memory_space=pl.ANY)
    return pl.pallas_call(
        functools.partial(_fox_dec_body, layer, n_pages, page),
        grid_spec=pltpu.PrefetchScalarGridSpec(
            num_scalar_prefetch=1, grid=(db,),
            in_specs=[blk(hw), blk(2 * hw), blk(LANES), any_spec, any_spec],
            out_specs=blk(hw),
            scratch_shapes=[pltpu.VMEM((2, hw, past), F32), pltpu.VMEM((2, hw, past), F32),
                            pltpu.VMEM((2, n_pages * H_FOX, page), F32),
                            pltpu.SemaphoreType.DMA((2,)), pltpu.SemaphoreType.DMA((2,)),
                            pltpu.SemaphoreType.DMA((2,))]),
        out_shape=jax.ShapeDtypeStruct((db, dt, hw), F32),
        compiler_params=_cparams(("arbitrary",)),
        name="fox_decode",
    )(page_table, fq, fox_new, fcum, cache_kvt, cache_lft)


def _cmp_weight_pairs(w_cmp):
    z = jnp.zeros((L_CMP, NSA_DIM, NSA_DIM), w_cmp.dtype)
    top = jnp.concatenate([w_cmp[0], z], axis=2)
    bot = jnp.concatenate([z, w_cmp[1]], axis=2)
    return jnp.concatenate([top, bot], axis=1).reshape(L_CMP * LANES, LANES).astype(BF16)


def _cmp_weight_paged(w, bpp):
    eye = jnp.eye(bpp, dtype=w.dtype)
    v = jnp.einsum("lde,pq->dplqe", w, eye)
    return v.reshape(NSA_DIM * bpp * L_CMP, bpp * NSA_DIM).astype(BF16)


def _nsa_dec_body(layer, n_pages, page, n_sel, lane_bids, pt_ref, nq_ref, new_ref, wnew_ref, misc_ref, wpast_ref,
                  wl_ref, vk_ref, vv_ref, e_ref, bid_ref, nsa_hbm, o_ref, cmp_buf, sel_buf, xk_scr, xv_scr,
                  sem_c, sem_s):
    dt = nq_ref.shape[1]
    past = n_pages * page
    nbp = past // L_CMP
    nb = nbp + 1
    nbl = e_ref.shape[0]
    rows = H_NSA * dt
    d2 = 2 * NSA_DIM
    slot = _gather_step(pt_ref, layer, n_pages, [(nsa_hbm, cmp_buf, sem_c, (0, d2), "mid", None),
                                                 (nsa_hbm, sel_buf, sem_s, (d2, d2), "lanes", page)])
    lane = lax.broadcasted_iota(jnp.int32, (dt, LANES), 1)
    q = _head_slabs(nq_ref[0], lane)[:, :NSA_DIM].astype(BF16)
    new = new_ref[0]
    for d in range(NSA_DIM):
        xk_scr[:, d * page:(d + 1) * page] = cmp_buf[slot, d].astype(BF16)
        xv_scr[:, d * page:(d + 1) * page] = cmp_buf[slot, NSA_DIM + d].astype(BF16)
    sk = _dot(xk_scr[...], vk_ref[...])
    sv = _dot(xv_scr[...], vv_ref[...])
    part = jnp.zeros((1, LANES), F32)
    for l in range(dt):
        part = part + _dot(jnp.broadcast_to(new[l:l + 1, :LANES], (8, LANES)).astype(BF16),
                           wl_ref[l * LANES:(l + 1) * LANES, :])[0:1]
    tail_row = lax.broadcasted_iota(jnp.int32, (nbl - nbp, LANES), 0)
    tail = jnp.where(tail_row == 0, jnp.broadcast_to(part, (nbl - nbp, LANES)), 0.0)
    kc16 = jnp.concatenate([sk[:, :NSA_DIM], sk[:, NSA_DIM:], tail[:, :NSA_DIM]], axis=0).astype(BF16)
    vc16 = jnp.concatenate([sv[:, :NSA_DIM], sv[:, NSA_DIM:], tail[:, NSA_DIM:]], axis=0).astype(BF16)
    bid4 = jnp.broadcast_to(bid_ref[...], (rows, nbl))
    qpos4 = past + lax.broadcasted_iota(jnp.int32, (rows, nbl), 0) % dt
    sc = jnp.where((bid4 + 1) * L_CMP - 1 <= qpos4, jnp.where(bid4 < nb, _dot_nt(q, kc16), -jnp.inf), -jnp.inf)
    pc = _softmax_rows(sc)
    oc = _dot(pc.astype(BF16), vc16)
    imp = pc[0:dt]
    for h in range(1, H_NSA):
        imp = imp + pc[h * dt:(h + 1) * dt]
    bid = jnp.broadcast_to(bid_ref[...], (dt, nbl))
    cur = (past + lax.broadcasted_iota(jnp.int32, (dt, nbl), 0)) // L_CMP
    imp, validf = _block_importance(imp, bid, cur, nb)
    sel = _select_blocks(imp, validf, bid, lane_bids, n_sel)
    seltok = _dot(sel.astype(BF16), e_ref[...])
    bias = jnp.where(seltok > 0.5, 0.0, -jnp.inf)
    kvt16 = sel_buf[slot].astype(BF16)
    kn16 = _pad_new(new[:, 2 * NSA_DIM:3 * NSA_DIM])
    vn16 = _pad_new(new[:, 3 * NSA_DIM:])
    new_lane = 2 * (nbp // 2) if nbp % 2 == 0 else None
    sel_new = jnp.concatenate([sel[:, new_lane:new_lane + 1]] * H_NSA, axis=0) > 0.5
    sa = _dot(q, kvt16[:NSA_DIM]) + jnp.concatenate([bias] * H_NSA, axis=0)
    sb = _dot_nt(q, kn16) + _new_bias(rows, dt, sel_new)
    osel = _joint_softmax_pv(sa, sb, kvt16[NSA_DIM:], vn16, va_t=True)
    wpt16 = wpast_ref[0, 0].astype(BF16)
    wn = wnew_ref[0]
    wb = wpt16.shape[1]
    qpw = past + lax.broadcasted_iota(jnp.int32, (rows, wb), 0) % dt
    kpw = past - wb + lax.broadcasted_iota(jnp.int32, (rows, wb), 1)
    okw = (qpw - kpw <= WINDOW) & (kpw >= 0)
    sa = jnp.where(okw, _dot(q, wpt16[:NSA_DIM]), -jnp.inf)
    sb = _dot_nt(q, _pad_new(wn[:, :NSA_DIM])) + _new_bias(rows, dt)
    ow = _joint_softmax_pv(sa, sb, wpt16[NSA_DIM:], _pad_new(wn[:, NSA_DIM:]), va_t=True)
    g = misc_ref[0]
    heads = []
    for h in range(H_NSA):
        sl = slice(h * dt, (h + 1) * dt)
        c0 = MISC_GATE + 3 * h
        heads.append(g[:, c0:c0 + 1] * oc[sl] + g[:, c0 + 1:c0 + 2] * osel[sl] + g[:, c0 + 2:c0 + 3] * ow[sl])
    o_ref[0] = jnp.concatenate(heads, axis=1).astype(o_ref.dtype)


def nsa_decode(layer, page_table, nq, nsa_new, win_new, misc, win_t, wl, vk, vv, cache_nsat):
    db, dt, _ = nq.shape
    n_pages, page = page_table.shape[1], cache_nsat.shape[3]
    past = n_pages * page
    assert page == 2 * L_CMP and dt <= min(L_CMP, WINDOW, NEW_PAD)
    wb = win_t.shape[3]
    nbp = past // L_CMP
    nb = nbp + 1
    nbl = -(-nb // LANES) * LANES
    n_sel = min(N_SEL, nb)
    bids = np.full((nbl,), nbl + nb, np.int32)
    bids[:n_pages] = 2 * np.arange(n_pages)
    bids[n_pages:2 * n_pages] = 2 * np.arange(n_pages) + 1
    bids[2 * n_pages] = nbp
    lane_bids = tuple((int(i), int(bids[i])) for i in range(2 * n_pages + 1))
    e = (jnp.arange(past)[None, :] // L_CMP == jnp.asarray(bids)[:, None]).astype(BF16)
    blk = lambda w: pl.BlockSpec((1, dt, w), lambda b, pt: (b, 0, 0))
    full2 = lambda a: pl.BlockSpec(a.shape, lambda b, pt: (0, 0))
    bid_arr = jnp.asarray(bids).reshape(1, nbl)
    return pl.pallas_call(
        functools.partial(_nsa_dec_body, layer, n_pages, page, n_sel, lane_bids),
        grid_spec=pltpu.PrefetchScalarGridSpec(
            num_scalar_prefetch=1, grid=(db,),
            in_specs=[blk(256), blk(256), blk(LANES), blk(LANES),
                      pl.BlockSpec((1, 1, 2 * NSA_DIM, wb), lambda b, pt: (layer, b, 0, 0)),
                      full2(wl), full2(vk), full2(vv), full2(e), full2(bid_arr),
                      pl.BlockSpec(memory_space=pl.ANY)],
            out_specs=blk(256),
            scratch_shapes=[pltpu.VMEM((2, 2 * NSA_DIM, n_pages, page), F32), pltpu.VMEM((2, 2 * NSA_DIM, past), F32),
                            pltpu.VMEM((n_pages, NSA_DIM * page), BF16), pltpu.VMEM((n_pages, NSA_DIM * page), BF16),
                            pltpu.SemaphoreType.DMA((2,)), pltpu.SemaphoreType.DMA((2,))]),
        out_shape=jax.ShapeDtypeStruct((db, dt, 256), F32),
        compiler_params=_cparams(("arbitrary",)),
        name="nsa_decode",
    )(page_table, nq, nsa_new, win_new, misc, win_t, wl, vk, vv, e, bid_arr, cache_nsat)


def _mixer_prompt(h, mods, g_pre, g_post, lw, consts):
    b, t, d = h.shape
    z = in_proj(h, mods, g_pre, lw, consts["tab_p"], lw["bvec"], consts["ltri_p"], consts["place"], decode=False)
    r3 = lambda a: a.reshape(b, t, a.shape[-1])
    o_mla = flash_pairs(r3(z["qn"]), r3(z["qx"]), r3(z["kn"]), r3(z["kx"]), r3(z["vm"]),
                        xw=ROPE_DIM, kx_shared=True)
    fkv16 = r3(z["fkv16"])
    o_fox = flash_pairs(r3(z["fq"]), r3(z["fqx"]), fkv16, r3(z["fkx"]), fkv16,
                        xw=FOX_XW, kx_shared=False, v_off=H_FOX // 2)
    nb = t // L_CMP
    cmpkv = compress_blocks(z["nsa"].reshape(b * nb, L_CMP * 256), lw["wbig"]).reshape(b, nb, LANES)
    cmpkv = jnp.pad(cmpkv, ((0, 0), (0, LANES - nb), (0, 0)))
    o_nsa = nsa_prompt(r3(z["nq"]), cmpkv, r3(z["nsa"]), r3(z["win"]), r3(z["misc"]))
    n = b * t
    h_new = out_proj(o_mla.reshape(n, -1), o_nsa.reshape(n, -1), o_fox.reshape(n, -1), lw["w_out"], h, mods, g_post)
    wb = min(WINDOW, t)
    misc = r3(z["misc"])
    state = (r3(z["ckv"]), misc[:, :, :ROPE_DIM], r3(z["nsa"]).reshape(b, t, 4, NSA_DIM),
             r3(z["win"])[:, t - wb:].reshape(b, wb, 2, NSA_DIM),
             r3(z["fox"]).reshape(b, t, 2, H_FOX, FOX_DIM), misc[:, :, MISC_LOGF:MISC_END])
    return h_new, state


def _mixer_sample(h, mods, g_pre, g_post, lw, consts, l, page_table, c_ckv, c_kr, c_nsa, s_win, c_fkv, c_flf):
    db, dt, d = h.shape
    z = in_proj(h, mods, g_pre, lw, consts["tab_s"], lw["bvec"], consts["ltri_s"], consts["place"], decode=True)
    r3 = lambda a: a.reshape(db, dt, a.shape[-1])
    misc = r3(z["misc"])
    o_lat = mla_decode(l, page_table, r3(z["qlat"]), r3(z["qx"]), r3(z["ckv"]), misc, c_ckv, c_kr)
    o_nsa = nsa_decode(l, page_table, r3(z["nq"]), r3(z["nsa"]), r3(z["win"]), misc, consts["win_t"], lw["wl"],
                       lw["vk"], lw["vv"], c_nsa)
    o_fox = fox_decode(l, page_table, r3(z["fq"]), r3(z["fox"]), r3(z["fcum"]), c_fkv, c_flf)
    win_new = jnp.concatenate([s_win[l, :, dt:], r3(z["win"]).reshape(db, dt, 2, NSA_DIM)], axis=1)
    n = db * dt
    h_new = out_proj(o_lat.reshape(n, -1), o_nsa.reshape(n, -1), o_fox.reshape(n, -1), lw["w_out"], h, mods, g_post,
                     wuv_bd=lw["wuv_bd"])
    state = (r3(z["ckv"]), misc[:, :, :ROPE_DIM], r3(z["nsa"]).reshape(db, dt, 4, NSA_DIM),
             win_new, r3(z["fox"]).reshape(db, dt, 2, H_FOX, FOX_DIM),
             misc[:, :, MISC_LOGF:MISC_END])
    return h_new, state


def _layer_weights(l, w_in, b_fox_f, mla_g_q, mla_g_kv, mla_w_uq, mla_w_uk, mla_w_uv, nsa_w_cmp, w_out):
    lw = _inproj_weights(w_in[l], mla_w_uq[l], mla_w_uk[l], mla_w_uv[l])
    bvec = jnp.zeros((1, LANES), F32).at[0, MISC_LOGF:MISC_END].set(b_fox_f[l])
    eye = jnp.eye(H_MLA, dtype=F32)
    wuv_bd = jnp.einsum("chd,hg->hcgd", mla_w_uv[l], eye).reshape(H_MLA * KV_LORA, H_MLA * V_DIM)
    lw.update(g_q=mla_g_q[l], g_kv=mla_g_kv[l], bvec=bvec, wbig=_cmp_weight(nsa_w_cmp[l]),
              w_out=w_out[l].astype(BF16), wuv_bd=wuv_bd.astype(BF16), wl=_cmp_weight_pairs(nsa_w_cmp[l]),
              vk=_cmp_weight_paged(nsa_w_cmp[l, 0], 2), vv=_cmp_weight_paged(nsa_w_cmp[l, 1], 2))
    return lw


def kernel(x_prompt, x_sample, cache_mla_ckv, cache_mla_krope, cache_nsa_kv, state_nsa_win, cache_fox_kv,
           cache_fox_logf, page_table, c_prompt, c_sample, ada_w, ada_b, norm_pre, norm_post, ffn_w_gate,
           ffn_w_up, ffn_w_down, w_in, b_fox_f, mla_g_q, mla_g_kv, mla_w_uq, mla_w_uk, mla_w_uv, nsa_w_cmp, w_out):
    depth = w_in.shape[0]
    b, t, d = x_prompt.shape
    db, dt, _ = x_sample.shape
    past = page_table.shape[1] * cache_mla_ckv.shape[2]
    mods_all = ada_mods(jnp.concatenate([c_prompt, c_sample], axis=0), ada_w, ada_b)
    wg16, wu16, wd16 = ffn_w_gate.astype(BF16), ffn_w_up.astype(BF16), ffn_w_down.astype(BF16)
    tm_p = min(t, 512)
    gb = min(db, max(1, 512 // dt))
    tm_s = gb * dt
    ii = jnp.arange(tm_s)
    consts = dict(
        tab_p=_rope_tables(jnp.arange(t, dtype=jnp.int32)),
        tab_s=jnp.tile(_rope_tables(past + jnp.arange(dt, dtype=jnp.int32)), (gb, 1)),
        ltri_p=(jnp.arange(tm_p)[:, None] >= jnp.arange(tm_p)[None, :]).astype(BF16),
        ltri_s=((ii[:, None] >= ii[None, :]) & (ii[:, None] // dt == ii[None, :] // dt)).astype(BF16),
        place=_fox_place(),
        win_t=jnp.transpose(state_nsa_win, (0, 1, 3, 4, 2)).reshape(depth, db, 2 * NSA_DIM, -1))
    pool, page = cache_mla_ckv.shape[1:3]
    cache_krt = jnp.transpose(cache_mla_krope, (0, 1, 3, 2))
    cache_nsat = jnp.transpose(cache_nsa_kv, (0, 1, 3, 4, 2)).reshape(depth, pool, 4 * NSA_DIM, page)
    cache_fkvt = jnp.transpose(cache_fox_kv, (0, 1, 3, 4, 5, 2)).reshape(depth, pool, 2 * H_FOX * FOX_DIM, page)
    cache_lft = jnp.transpose(cache_fox_logf, (0, 1, 3, 2))
    hp, hs = x_prompt, x_sample
    st_p, st_s = [], []
    for l in range(depth):
        lw = _layer_weights(l, w_in, b_fox_f, mla_g_q, mla_g_kv, mla_w_uq, mla_w_uk, mla_w_uv, nsa_w_cmp, w_out)
        mp = mods_all[l, :, :b].reshape(3 * N_SUB, b, 1, d)
        ms = mods_all[l, :, b:].reshape(3 * N_SUB, db, 1, d)
        hp = ffn_half(hp, mp, 0, norm_pre[l, 0], norm_post[l, 0], wg16, wu16, wd16, l, 0)
        hs = ffn_half(hs, ms, 0, norm_pre[l, 0], norm_post[l, 0], wg16, wu16, wd16, l, 0)
        hp, sp = _mixer_prompt(hp, mp, norm_pre[l, 1], norm_post[l, 1], lw, consts)
        hs, ss = _mixer_sample(hs, ms, norm_pre[l, 1], norm_post[l, 1], lw, consts, l, page_table,
                               cache_mla_ckv, cache_krt, cache_nsat, state_nsa_win, cache_fkvt, cache_lft)
        hp = ffn_half(hp, mp, 2, norm_pre[l, 2], norm_post[l, 2], wg16, wu16, wd16, l, 1)
        hs = ffn_half(hs, ms, 2, norm_pre[l, 2], norm_post[l, 2], wg16, wu16, wd16, l, 1)
        st_p.append(sp)
        st_s.append(ss)
    outs_p = tuple(jnp.stack(a) for a in zip(*st_p))
    outs_s = tuple(jnp.stack(a) for a in zip(*st_s))
    return (hp, hs) + outs_p + outs_s
```

```python
import functools

import numpy as np
import jax
import jax.numpy as jnp
from jax import lax
from jax.experimental import pallas as pl
from jax.experimental.pallas import tpu as pltpu

F32, BF16 = jnp.float32, jnp.bfloat16

H_MLA, Q_LORA, KV_LORA, NOPE_DIM, ROPE_DIM, V_DIM = 8, 384, 256, 64, 32, 64
H_NSA, NSA_DIM, L_CMP, N_SEL, WINDOW = 4, 64, 64, 16, 512
H_FOX, FOX_DIM = 4, 64
N_SUB = 3
ROPE_THETA = 10000.0
FORCED_SCORE = 1.0e4
EPS = 1e-6
MLA_SCALE = (NOPE_DIM + ROPE_DIM) ** -0.5
NSA_SCALE = NSA_DIM ** -0.5
FOX_SCALE = FOX_DIM ** -0.5
IN_SPLITS = (Q_LORA, KV_LORA, ROPE_DIM, H_NSA * NSA_DIM, 6 * NSA_DIM, H_NSA * 3,
             H_FOX * FOX_DIM, H_FOX * FOX_DIM, H_FOX * FOX_DIM, H_FOX)

LANES = 128
VMEM_LIMIT = 56 * 1024 * 1024

MISC_GATE = ROPE_DIM
MISC_LOGF = ROPE_DIM + H_NSA * 3
MISC_END = MISC_LOGF + H_FOX
FOX_XW = 6

_W_WIDTHS = (("cq", Q_LORA), ("ckv", KV_LORA), ("nq", 256), ("nqs", 256), ("nkv", 384), ("nkvs", 384),
             ("fq", 256), ("fkv", 512), ("misc", 128), ("miscs", 128), ("krp", 128), ("krps", 128))
_W_OFF = {}
_acc = 0
for _n, _w in _W_WIDTHS:
    _W_OFF[_n] = (_acc, _acc + _w)
    _acc += _w
W_TOTAL = _acc
_TAB = {n: (i * LANES, (i + 1) * LANES) for i, n in
        enumerate(("ckx", "skx", "cm", "sm", "cn", "sn", "cnkv", "snkv"))}
TAB_TOTAL = 8 * LANES


def _cparams(sem):
    return pltpu.CompilerParams(dimension_semantics=sem, vmem_limit_bytes=VMEM_LIMIT)


def _rms(x, g):
    ms = jnp.mean(x * x, axis=-1, keepdims=True)
    return x * lax.rsqrt(ms + EPS) * g


def _silu(x):
    return x / (1.0 + jnp.exp(-x))


def _softmax_rows(s):
    m = jnp.max(s, axis=-1, keepdims=True)
    m = jnp.where(m > -jnp.inf, m, 0.0)
    e = jnp.exp(s - m)
    d = jnp.sum(e, axis=-1, keepdims=True)
    return e * (1.0 / jnp.where(d > 0, d, 1.0))


def _softmax_parts(s):
    m = jnp.max(s, axis=-1, keepdims=True)
    m = jnp.where(m > -jnp.inf, m, 0.0)
    e = jnp.exp(s - m)
    d = jnp.sum(e, axis=-1, keepdims=True)
    return e, 1.0 / jnp.where(d > 0, d, 1.0)


def _dot(a, b):
    return jnp.dot(a, b, preferred_element_type=F32)


def _dot_nt(a, b):
    return lax.dot_general(a, b, (((1,), (1,)), ((), ())), preferred_element_type=F32)


def _split3(x):
    hi = x.astype(BF16)
    r1 = x - hi.astype(F32)
    mid = r1.astype(BF16)
    lo = (r1 - mid.astype(F32)).astype(BF16)
    return hi, mid, lo


def _ada_body(c_ref, w_ref, b_ref, o_ref):
    s = _silu(c_ref[...])
    o_ref[0, 0] = _dot(s.astype(BF16), w_ref[0].astype(BF16)) + b_ref[0, 0]


def ada_mods(c_all, ada_w, ada_b):
    depth, d, n9 = ada_w.shape
    nc = n9 // d
    bt = c_all.shape[0]
    return pl.pallas_call(
        _ada_body,
        grid=(depth, nc),
        in_specs=[pl.BlockSpec((bt, d), lambda l, n: (0, 0)),
                  pl.BlockSpec((1, d, d), lambda l, n: (l, 0, n)),
                  pl.BlockSpec((1, 1, 1, d), lambda l, n: (l, n, 0, 0))],
        out_specs=pl.BlockSpec((1, 1, bt, d), lambda l, n: (l, n, 0, 0)),
        out_shape=jax.ShapeDtypeStruct((depth, nc, bt, d), F32),
        compiler_params=_cparams(("arbitrary", "arbitrary")),
        name="ada_mods",
    )(c_all, ada_w, ada_b.reshape(depth, nc, 1, d))


def _ffn_body(h_ref, m_ref, gpre_ref, gpost_ref, wg_ref, wu_ref, wd_ref, o_ref, u_scr, acc_scr):
    f = pl.program_id(1)
    g_, r_, d_ = h_ref.shape

    @pl.when(f == 0)
    def _():
        y = _rms(h_ref[...], gpre_ref[...])
        u = y * (1.0 + m_ref[1]) + m_ref[0]
        u_scr[...] = u.reshape(g_ * r_, d_).astype(BF16)
        acc_scr[...] = jnp.zeros_like(acc_scr)

    u = u_scr[...]
    gate = _dot(u, wg_ref[0, 0])
    up = _dot(u, wu_ref[0, 0])
    a = (_silu(gate) * up).astype(BF16)
    acc_scr[...] += _dot(a, wd_ref[0, 0])

    @pl.when(f == pl.num_programs(1) - 1)
    def _():
        yn = _rms(acc_scr[...], gpost_ref[...]).reshape(g_, r_, d_)
        o_ref[...] = h_ref[...] + 0.5 * m_ref[2] * yn


def _tok_tile(n_rows_per_group, cap):
    return min(n_rows_per_group, cap)


def ffn_half(h, mods, sub, g_pre, g_post, wg, wu, wd, layer, half, *, tm_cap=1024, tf=256):
    b, t, d = h.shape
    dff = wg.shape[-1]
    tf = min(tf, dff)
    if t >= 8 * 16:
        tm = _tok_tile(t, tm_cap)
        blk, tpb = (1, tm, d), t // tm
        hmap = lambda i, f: (i // tpb, i % tpb, 0)
        mmap = lambda i, f: (sub, i // tpb, 0, 0)
        grid0, mblk = b * tpb, (3, 1, 1, d)
    else:
        gb = min(b, max(1, tm_cap // t))
        blk = (gb, t, d)
        hmap = lambda i, f: (i, 0, 0)
        mmap = lambda i, f: (sub, i, 0, 0)
        grid0, mblk = b // gb, (3, gb, 1, d)
    rows = blk[0] * blk[1]
    return pl.pallas_call(
        _ffn_body,
        grid=(grid0, dff // tf),
        in_specs=[pl.BlockSpec(blk, hmap),
                  pl.BlockSpec(mblk, mmap),
                  pl.BlockSpec((1, d), lambda i, f: (0, 0)),
                  pl.BlockSpec((1, d), lambda i, f: (0, 0)),
                  pl.BlockSpec((1, 1, d, tf), lambda i, f: (layer, half, 0, f)),
                  pl.BlockSpec((1, 1, d, tf), lambda i, f: (layer, half, 0, f)),
                  pl.BlockSpec((1, 1, tf, d), lambda i, f: (layer, half, f, 0))],
        out_specs=pl.BlockSpec(blk, hmap),
        out_shape=jax.ShapeDtypeStruct(h.shape, F32),
        scratch_shapes=[pltpu.VMEM((rows, d), BF16), pltpu.VMEM((rows, d), F32)],
        compiler_params=_cparams(("arbitrary", "arbitrary")),
        name="ffn_half",
    )(h, mods, g_pre.reshape(1, d), g_post.reshape(1, d), wg, wu, wd)


def _swap_halves(w, dh):
    k = w.shape[0]
    w4 = w.reshape(k, w.shape[1] // dh, 2, dh // 2)
    return w4[:, :, ::-1, :].reshape(k, -1)


def _inproj_weights(w_in, w_uq, w_uk, w_uv):
    k = w_in.shape[0]
    cuts = np.cumsum(IN_SPLITS)[:-1].tolist()
    cq, ckv, kr, nq, nkv, ng, fq, fk, fv, ff = jnp.split(w_in, cuts, axis=1)
    z = lambda n: jnp.zeros((k, n), w_in.dtype)
    nkv3 = nkv.reshape(k, 3, 2, NSA_DIM)
    nkvs = jnp.stack([_swap_halves(nkv3[:, :, 0].reshape(k, -1), NSA_DIM).reshape(k, 3, NSA_DIM),
                      jnp.zeros((k, 3, NSA_DIM), w_in.dtype)], axis=2).reshape(k, -1)
    krs = _swap_halves(kr, ROPE_DIM)
    cols = dict(cq=cq, ckv=ckv, nq=nq, nqs=_swap_halves(nq, NSA_DIM), nkv=nkv, nkvs=nkvs, fq=fq,
                fkv=jnp.concatenate([fk, fv], axis=1),
                misc=jnp.concatenate([kr, ng, ff, z(LANES - MISC_END)], axis=1),
                miscs=jnp.concatenate([krs, z(LANES - ROPE_DIM)], axis=1),
                krp=jnp.concatenate([kr, kr, z(LANES - 2 * ROPE_DIM)], axis=1),
                krps=jnp.concatenate([krs, krs, z(LANES - 2 * ROPE_DIM)], axis=1))
    w_all = jnp.concatenate([cols[n] for n, _ in _W_WIDTHS], axis=1).astype(BF16)
    dq = NOPE_DIM + ROPE_DIM
    uq = w_uq.reshape(Q_LORA, H_MLA, dq)
    wqn = uq[:, :, :NOPE_DIM].reshape(Q_LORA, -1)
    rope = uq[:, :, NOPE_DIM:]
    zr = jnp.zeros((Q_LORA, H_MLA // 2, LANES - 2 * ROPE_DIM), w_uq.dtype)

    def pairs(r):
        return jnp.concatenate([r.reshape(Q_LORA, H_MLA // 2, 2 * ROPE_DIM), zr], axis=2).reshape(Q_LORA, -1)

    wqr = pairs(rope)
    wqs = pairs(_swap_halves(rope.reshape(Q_LORA, -1), ROPE_DIM).reshape(Q_LORA, H_MLA, ROPE_DIM))
    wuk = w_uk.reshape(KV_LORA, -1)
    wuv = w_uv.reshape(KV_LORA, -1)
    ukt = jnp.transpose(w_uk, (1, 2, 0))
    zt = jnp.zeros_like(ukt)
    even = jnp.concatenate([ukt, zt], axis=1)
    odd = jnp.concatenate([zt, ukt], axis=1)
    wukt = jnp.where((jnp.arange(H_MLA) % 2 == 0)[:, None, None], even, odd)
    return dict(w_all=w_all, wqn=wqn.astype(BF16), wqr=wqr.astype(BF16), wqs=wqs.astype(BF16),
                wuk=wuk.astype(BF16), wuv=wuv.astype(BF16), wukt=wukt.astype(BF16))


def _rope_tables(pos):
    def cs(dh):
        half = dh // 2
        inv = jnp.power(ROPE_THETA, -jnp.arange(half, dtype=F32) / half)
        ang = pos.astype(F32)[:, None] * inv[None, :]
        c, s = jnp.cos(ang), jnp.sin(ang)
        return jnp.concatenate([c, c], 1), jnp.concatenate([-s, s], 1)

    n = pos.shape[0]
    c32, s32 = cs(ROPE_DIM)
    c64, s64 = cs(NSA_DIM)
    one, zero = jnp.ones((n, 1), F32), jnp.zeros((n, 1), F32)
    rep = lambda a, k: jnp.tile(a, (1, k))
    tabs = dict(
        ckx=jnp.concatenate([c32, c32, rep(zero, 64)], 1), skx=jnp.concatenate([s32, s32, rep(zero, 64)], 1),
        cm=jnp.concatenate([c32, rep(one, 96)], 1), sm=jnp.concatenate([s32, rep(zero, 96)], 1),
        cn=NSA_SCALE * jnp.concatenate([c64, c64], 1), sn=NSA_SCALE * jnp.concatenate([s64, s64], 1),
        cnkv=jnp.concatenate([c64, rep(one, 64)], 1), snkv=jnp.concatenate([s64, rep(zero, 64)], 1))
    return jnp.concatenate([tabs[k] for k in _TAB], axis=1)


def _fox_place():
    pq = np.zeros((3 * LANES, 2 * LANES), np.float32)
    pk = np.zeros((3 * LANES, 2 * LANES), np.float32)
    cq = np.zeros((1, 2 * LANES), np.float32)
    ck = np.zeros((1, 2 * LANES), np.float32)
    for h in range(H_FOX):
        base = (h // 2) * LANES + (h % 2) * FOX_XW
        for s in range(3):
            pq[s * LANES + MISC_LOGF + h, base + s] = 1.0
            pk[s * LANES + MISC_LOGF + h, base + 3 + s] = -1.0
            cq[0, base + 3 + s] = 1.0
            ck[0, base + s] = 1.0
    return (jnp.asarray(pq, BF16), jnp.asarray(pk, BF16), jnp.asarray(cq), jnp.asarray(ck))


def _inproj_body(decode, tpb, *refs):
    (h_ref, m_ref, gpre_ref, w_ref, gq_ref, gkv_ref, wqn_ref, wqr_ref, wqs_ref, wuk_ref, wuv_ref, wukt_ref,
     tab_ref, bvec_ref, ltri_ref, pq_ref, pk_ref, cq_ref, ck_ref) = refs[:19]
    outs = refs[19:-1]
    carry_scr = refs[-1]
    (ckv_o, nsa_o, win_o, fox_o, misc_o, qn_o, qx_o, kx_o, nq_o, fq_o, fkv16_o) = outs[:11]
    g_, r_, d_ = h_ref.shape
    tm = g_ * r_
    y = _rms(h_ref[...], gpre_ref[...])
    u = (y * (1.0 + m_ref[1]) + m_ref[0]).reshape(tm, d_).astype(BF16)

    def z(name):
        a, b = _W_OFF[name]
        return _dot(u, w_ref[:, a:b])

    def tab(name, k=1):
        a, b = _TAB[name]
        t = tab_ref[:, a:b]
        return t if k == 1 else jnp.concatenate([t] * k, axis=1)

    cqn = _rms(z("cq"), gq_ref[...]).astype(BF16)
    qn = _dot(cqn, wqn_ref[...]) * MLA_SCALE
    qn16 = qn.astype(BF16)
    qn_o[...] = qn16.astype(qn_o.dtype)
    qx = (_dot(cqn, wqr_ref[...]) * tab("ckx", 4) + _dot(cqn, wqs_ref[...]) * tab("skx", 4)) * MLA_SCALE
    qx_o[...] = qx.astype(qx_o.dtype)
    ckvn = _rms(z("ckv"), gkv_ref[...])
    ckv_o[...] = ckvn
    kx_o[...] = (z("krp") * tab("ckx") + z("krps") * tab("skx")).astype(kx_o.dtype)
    if decode:
        qlat_o, fcum_o = outs[11:]
        for h in range(H_MLA):
            p = h // 2
            qlat_o[:, h * KV_LORA:(h + 1) * KV_LORA] = _dot(qn16[:, p * LANES:(p + 1) * LANES], wukt_ref[h])
    else:
        kn_o, vm_o, fqx_o, fkx_o = outs[11:]
        ck16 = ckvn.astype(BF16)
        kn_o[...] = _dot(ck16, wuk_ref[...]).astype(BF16)
        vm_o[...] = _dot(ck16, wuv_ref[...]).astype(BF16)
    nq_o[...] = (z("nq") * tab("cn", 2) + z("nqs") * tab("sn", 2)).astype(nq_o.dtype)
    nkv = z("nkv") * tab("cnkv", 3) + z("nkvs") * tab("snkv", 3)
    nsa_o[...] = nkv[:, :4 * NSA_DIM]
    win_o[...] = nkv[:, 4 * NSA_DIM:]
    fq_o[...] = (z("fq") * FOX_SCALE).astype(fq_o.dtype)
    fkv = z("fkv")
    fox_o[...] = fkv
    fkv16_o[...] = fkv.astype(fkv16_o.dtype)
    zm = z("misc")
    lane = lax.broadcasted_iota(jnp.int32, (tm, LANES), 1)
    roped = zm * tab("cm") + z("miscs") * tab("sm")
    xb = zm + bvec_ref[...]
    logsig = jnp.minimum(xb, 0.0) - jnp.log(1.0 + jnp.exp(-jnp.abs(xb)))
    misc = jnp.where(lane < MISC_GATE, roped,
                     jnp.where(lane < MISC_LOGF, 1.0 / (1.0 + jnp.exp(-zm)),
                               jnp.where(lane < MISC_END, logsig, 0.0)))
    misc_o[...] = misc
    lf = jnp.where((lane >= MISC_LOGF) & (lane < MISC_END), misc, 0.0)
    ltri = ltri_ref[...]
    hi, mid, lo = _split3(lf)
    cum = _dot(ltri, hi) + _dot(ltri, mid) + _dot(ltri, lo)
    first = (pl.program_id(0) % tpb) == 0
    carry = jnp.where(first, 0.0, carry_scr[0:1, :])
    fc = cum + carry
    carry_scr[0:1, :] = fc[tm - 1:tm, :]
    if decode:
        fcum_o[...] = fc
    else:
        g3 = jnp.concatenate(list(_split3(fc)), axis=1)
        fqx_o[...] = (_dot(g3, pq_ref[...]) + cq_ref[...]).astype(BF16)
        fkx_o[...] = (_dot(g3, pk_ref[...]) + ck_ref[...]).astype(BF16)


def in_proj(h, mods, g_pre, lw, tables, bvec, ltri, place, *, decode, tm_cap=512):
    b, t, d = h.shape
    n = b * t
    if not decode:
        tm = _tok_tile(t, tm_cap)
        tpb = t // tm
        blk = (1, tm, d)
        hmap = lambda i: (i // tpb, i % tpb, 0)
        mmap = lambda i: (1, i // tpb, 0, 0)
        tmap = lambda i: (i % tpb, 0)
        grid0, mblk = b * tpb, (3, 1, 1, d)
    else:
        gb = min(b, max(1, tm_cap // t))
        tm, tpb = gb * t, 1
        blk = (gb, t, d)
        hmap = lambda i: (i, 0, 0)
        mmap = lambda i: (1, i, 0, 0)
        tmap = lambda i: (0, 0)
        grid0, mblk = b // gb, (3, gb, 1, d)
    full = lambda a: pl.BlockSpec(a.shape, lambda i: (0,) * a.ndim)
    pq, pk, cq, ck = place
    ins = [h, mods, g_pre.reshape(1, d), lw["w_all"], lw["g_q"].reshape(1, -1), lw["g_kv"].reshape(1, -1),
           lw["wqn"], lw["wqr"], lw["wqs"], lw["wuk"], lw["wuv"], lw["wukt"], tables, bvec, ltri, pq, pk, cq, ck]
    in_specs = [pl.BlockSpec(blk, hmap), pl.BlockSpec(mblk, mmap)] + [full(a) for a in ins[2:12]] + \
               [pl.BlockSpec((tm, TAB_TOTAL), tmap)] + [full(a) for a in ins[13:]]
    names = ["ckv", "nsa", "win", "fox", "misc", "qn", "qx", "kx", "nq", "fq", "fkv16"]
    widths = [KV_LORA, 256, 128, 512, 128, 512, 512, 128, 256, 256, 512]
    dts = [F32] * 5 + [F32 if decode else BF16] * 6
    if decode:
        names += ["qlat", "fcum"]
        widths += [H_MLA * KV_LORA, 128]
        dts += [F32, F32]
    else:
        names += ["kn", "vm", "fqx", "fkx"]
        widths += [512, 512, 256, 256]
        dts += [BF16, BF16, BF16, BF16]
    outs = pl.pallas_call(
        functools.partial(_inproj_body, decode, tpb),
        grid=(grid0,),
        in_specs=in_specs,
        out_specs=[pl.BlockSpec((tm, w), lambda i: (i, 0)) for w in widths],
        out_shape=[jax.ShapeDtypeStruct((n, w), dt) for w, dt in zip(widths, dts)],
        scratch_shapes=[pltpu.VMEM((8, LANES), F32)],
        compiler_params=_cparams(("arbitrary",)),
        name="in_proj_dec" if decode else "in_proj",
    )(*ins)
    return dict(zip(names, outs))


def _flash_body(xw, tq, tk, nsplit, qa_ref, qx_ref, ka_ref, kx_ref, v_ref, o_ref, m_scr, l_scr, acc_scr):
    i = pl.program_id(2)
    lane = lax.broadcasted_iota(jnp.int32, (tq, LANES), 1)
    qa, qx = qa_ref[0], qx_ref[0]
    zero = jnp.zeros_like(qa)
    q0 = jnp.concatenate([jnp.where(lane < 64, qa, zero), jnp.where(lane < xw, qx, zero)], axis=1)
    q1 = jnp.concatenate([jnp.where(lane >= 64, qa, zero),
                          jnp.where((lane >= xw) & (lane < 2 * xw), qx, zero)], axis=1)
    q2 = jnp.concatenate([q0, q1], axis=0)
    m_scr[...] = jnp.full_like(m_scr, -jnp.inf)
    l_scr[...] = jnp.zeros_like(l_scr)
    acc_scr[...] = jnp.zeros_like(acc_scr)

    def chunk(c, masked):
        k0 = pl.multiple_of(c * tk, tk)
        k2 = jnp.concatenate([ka_ref[0, pl.ds(k0, tk), :], kx_ref[0, pl.ds(k0, tk), :]], axis=1)
        v = v_ref[0, pl.ds(k0, tk), :]
        tr = tq // nsplit
        if masked:
            row = lax.broadcasted_iota(jnp.int32, (tr, tk), 0)
            col = lax.broadcasted_iota(jnp.int32, (tr, tk), 1)
        for ch in range(2 * nsplit):
            rs = slice(ch * tr, (ch + 1) * tr)
            s = _dot_nt(q2[rs], k2)
            if masked:
                s = jnp.where(k0 + col <= i * tq + (ch % nsplit) * tr + row, s, -jnp.inf)
            m_prev = m_scr[rs]
            m_new = jnp.maximum(m_prev, jnp.max(s, axis=1, keepdims=True))
            alpha = jnp.exp(m_prev - m_new)
            p = jnp.exp(s - m_new[:, :1])
            l_scr[rs] = alpha * l_scr[rs] + jnp.sum(p, axis=1, keepdims=True)
            acc_scr[rs] = alpha * acc_scr[rs] + _dot(p.astype(BF16), v)
            m_scr[rs] = m_new

    n_full = (i * tq) // tk

    def body(c, carry):
        chunk(c, False)
        return carry

    lax.fori_loop(0, n_full, body, 0)
    chunk(n_full, True)
    o = acc_scr[...] * (1.0 / l_scr[...])
    o_ref[0] = jnp.where(lane < 64, o[:tq], o[tq:]).astype(o_ref.dtype)


def flash_pairs(qa, qx, ka, kx, v, *, xw, kx_shared, v_off=0, tq=512, tk=512):
    b, t, w = qa.shape
    npair = w // LANES
    tq, tk = min(tq, t), min(tk, t)
    kxmap = (lambda bb, p, i: (bb, 0, 0)) if kx_shared else (lambda bb, p, i: (bb, 0, p))
    return pl.pallas_call(
        functools.partial(_flash_body, xw, tq, tk, 1),
        grid=(b, npair, t // tq),
        in_specs=[pl.BlockSpec((1, tq, LANES), lambda bb, p, i: (bb, i, p)),
                  pl.BlockSpec((1, tq, LANES), lambda bb, p, i: (bb, i, p)),
                  pl.BlockSpec((1, t, LANES), lambda bb, p, i: (bb, 0, p)),
                  pl.BlockSpec((1, t, LANES), kxmap),
                  pl.BlockSpec((1, t, LANES), lambda bb, p, i: (bb, 0, v_off + p))],
        out_specs=pl.BlockSpec((1, tq, LANES), lambda bb, p, i: (bb, i, p)),
        out_shape=jax.ShapeDtypeStruct((b, t, w), BF16),
        scratch_shapes=[pltpu.VMEM((2 * tq, LANES), F32)] * 3,
        compiler_params=_cparams(("arbitrary", "arbitrary", "arbitrary")),
        name="flash_pairs",
    )(qa, qx, ka, kx, v)


def _compress_body(x_ref, w_ref, o_ref):
    @pl.when(pl.program_id(0) == 0)
    def _():
        o_ref[...] = jnp.zeros_like(o_ref)

    o_ref[...] += _dot(x_ref[...].astype(BF16), w_ref[...])


def _cmp_weight(w_cmp):
    z = jnp.zeros((L_CMP, NSA_DIM, NSA_DIM), w_cmp.dtype)
    top = jnp.concatenate([w_cmp[0], z], axis=2)
    bot = jnp.concatenate([z, w_cmp[1]], axis=2)
    zz = jnp.zeros((L_CMP, 2 * NSA_DIM, 2 * NSA_DIM), w_cmp.dtype)
    return jnp.concatenate([top, bot, zz], axis=1).reshape(L_CMP * 4 * NSA_DIM, 2 * NSA_DIM).astype(BF16)


def compress_blocks(nsa_rows, wbig, *, tk=2048):
    nblk, kk = nsa_rows.shape
    tk = min(tk, kk)
    return pl.pallas_call(
        _compress_body,
        grid=(kk // tk,),
        in_specs=[pl.BlockSpec((nblk, tk), lambda k: (0, k)), pl.BlockSpec((tk, LANES), lambda k: (k, 0))],
        out_specs=pl.BlockSpec((nblk, LANES), lambda k: (0, 0)),
        out_shape=jax.ShapeDtypeStruct((nblk, LANES), F32),
        compiler_params=_cparams(("arbitrary",)),
        name="compress_blocks",
    )(nsa_rows, wbig)


def _head_slabs(qf, lane):
    slabs = []
    for p in range(H_NSA // 2):
        x = qf[:, p * LANES:(p + 1) * LANES]
        slabs.append(jnp.where(lane < 64, x, 0.0))
        slabs.append(jnp.where(lane < 64, pltpu.roll(x, 64, 1), 0.0))
    return jnp.concatenate(slabs, axis=0)


def _select_blocks(imp, validf, bid, lane_bids, n_sel):
    rank = jnp.zeros(imp.shape, F32)
    for j, bj in lane_bids:
        col = imp[:, j:j + 1]
        tie = jnp.where(bid > bj, 1.0, 0.0)
        rank = rank + jnp.where(col > imp, 1.0, jnp.where(col == imp, tie, 0.0))
    return jnp.where(rank < n_sel, validf, 0.0)


def _block_importance(imp, bid, cur, nb):
    imp = jnp.where(bid == 0, FORCED_SCORE, jnp.where(bid == cur, FORCED_SCORE,
                                                      jnp.where(bid == cur - 1, FORCED_SCORE, imp)))
    validf = jnp.where(bid <= cur, jnp.where(bid < nb, 1.0, 0.0), 0.0)
    return jnp.where(validf > 0.5, imp, -jnp.inf), validf


def _nsa_prompt_body(tq, t, nb, span, n_sel, kstep, *refs):
    i = pl.program_id(1)
    need = ((i + 1) * tq + kstep - 1) // kstep
    for v in range(1, t // kstep + 1):
        @pl.when(need == v)
        def _():
            _nsa_prompt_tile(v * kstep, tq, t, nb, span, n_sel, i, *refs)


def _nsa_prompt_tile(klen, tq, t, nb, span, n_sel, i, q_ref, ckv_ref, kv_ref, win_ref, misc_ref, e_ref, o_ref):
    lane = lax.broadcasted_iota(jnp.int32, (tq, LANES), 1)
    q = _head_slabs(q_ref[0].astype(F32), lane).astype(BF16)
    rows = H_NSA * tq
    row4 = lax.broadcasted_iota(jnp.int32, (rows, LANES), 0)
    qpos4 = i * tq + (row4 & (tq - 1))
    bi4 = lax.broadcasted_iota(jnp.int32, (rows, LANES), 1)
    ckv16 = ckv_ref[0].astype(BF16)
    sc = jnp.where((bi4 + 1) * L_CMP - 1 <= qpos4, jnp.where(bi4 < nb, _dot_nt(q, ckv16), -jnp.inf), -jnp.inf)
    pc = _softmax_rows(sc)
    oc = _dot(pc.astype(BF16), ckv16)
    imp = pc[0:tq] + pc[tq:2 * tq] + pc[2 * tq:3 * tq] + pc[3 * tq:4 * tq]
    qpos1 = i * tq + lax.broadcasted_iota(jnp.int32, (tq, LANES), 0)
    imp, validf = _block_importance(imp, lane, qpos1 // L_CMP, nb)
    sel = _select_blocks(imp, validf, lane, tuple((j, j) for j in range(nb)), n_sel)
    seltok = _dot(sel.astype(BF16), e_ref[:, :klen])
    kpos = lax.broadcasted_iota(jnp.int32, (tq, klen), 1)
    qpos_t = i * tq + lax.broadcasted_iota(jnp.int32, (tq, klen), 0)
    bias = jnp.where(kpos <= qpos_t, jnp.where(seltok > 0.5, 0.0, -jnp.inf), -jnp.inf)
    kv16 = kv_ref[0, :klen, :].astype(BF16)
    es, rs = _softmax_parts(_dot_nt(q, kv16) + jnp.concatenate([bias] * H_NSA, axis=0))
    osel = _dot(es.astype(BF16), kv16) * rs
    start = jnp.clip(i * tq - WINDOW, 0, t - span)
    start = pl.multiple_of(start, 8)
    w16 = win_ref[0, pl.ds(start, span), :].astype(BF16)
    roww = lax.broadcasted_iota(jnp.int32, (rows, span), 0)
    diff = i * tq + (roww & (tq - 1)) - (start + lax.broadcasted_iota(jnp.int32, (rows, span), 1))
    ew, rw = _softmax_parts(jnp.where(diff >= 0, jnp.where(diff <= WINDOW, _dot_nt(q, w16), -jnp.inf), -jnp.inf))
    ow = _dot(ew.astype(BF16), w16) * rw
    g = misc_ref[0]
    heads = []
    for h in range(H_NSA):
        sl = slice(h * tq, (h + 1) * tq)
        c0 = MISC_GATE + 3 * h
        heads.append(g[:, c0:c0 + 1] * oc[sl] + g[:, c0 + 1:c0 + 2] * osel[sl] + g[:, c0 + 2:c0 + 3] * ow[sl])
    pairs = [jnp.where(lane < 64, pltpu.roll(heads[2 * p], 64, 1), heads[2 * p + 1]) for p in range(H_NSA // 2)]
    o_ref[0] = jnp.concatenate(pairs, axis=1).astype(o_ref.dtype)


def nsa_prompt(nq, cmpkv, nsa, win, misc, *, tq=128):
    b, t, _ = nq.shape
    nb = -(-t // L_CMP)
    tq = min(tq, t)
    span = min(t, WINDOW + tq)
    n_sel = min(N_SEL, nb)
    e = (jnp.arange(t)[None, :] // L_CMP == jnp.arange(LANES)[:, None]).astype(BF16)
    kstep = min(t, max(tq, 512))
    return pl.pallas_call(
        functools.partial(_nsa_prompt_body, tq, t, nb, span, n_sel, kstep),
        grid=(b, t // tq),
        in_specs=[pl.BlockSpec((1, tq, 256), lambda bb, i: (bb, i, 0)),
                  pl.BlockSpec((1, LANES, LANES), lambda bb, i: (bb, 0, 0)),
                  pl.BlockSpec((1, t, LANES), lambda bb, i: (bb, 0, 1)),
                  pl.BlockSpec((1, t, LANES), lambda bb, i: (bb, 0, 0)),
                  pl.BlockSpec((1, tq, LANES), lambda bb, i: (bb, i, 0)),
                  pl.BlockSpec((LANES, t), lambda bb, i: (0, 0))],
        out_specs=pl.BlockSpec((1, tq, 256), lambda bb, i: (bb, i, 0)),
        out_shape=jax.ShapeDtypeStruct((b, t, 256), BF16),
        compiler_params=_cparams(("arbitrary", "arbitrary")),
        name="nsa_prompt",
    )(nq, cmpkv, nsa, win, misc, e)


def _outproj_body(decode, *refs):
    if decode:
        ol_ref, wuvb_ref, on_ref, of_ref, w_ref, h_ref, m_ref, gpost_ref, o_ref = refs
        om = _dot(ol_ref[...].astype(BF16), wuvb_ref[...]).astype(BF16)
    else:
        om_ref, on_ref, of_ref, w_ref, h_ref, m_ref, gpost_ref, o_ref = refs
        om = om_ref[...]
    g_, r_, d_ = h_ref.shape
    o = jnp.concatenate([om, on_ref[...].astype(BF16), of_ref[...].astype(BF16)], axis=1)
    yn = _rms(_dot(o, w_ref[...]), gpost_ref[...]).reshape(g_, r_, d_)
    o_ref[...] = h_ref[...] + m_ref[2] * yn


def out_proj(o_mla, o_nsa, o_fox, w_out, h, mods, g_post, *, wuv_bd=None, tm_cap=512):
    b, t, d = h.shape
    decode = wuv_bd is not None
    if not decode:
        tm = _tok_tile(t, tm_cap)
        tpb = t // tm
        blk = (1, tm, d)
        hmap = lambda i: (i // tpb, i % tpb, 0)
        mmap = lambda i: (1, i // tpb, 0, 0)
        grid0, mblk = b * tpb, (3, 1, 1, d)
    else:
        gb = min(b, max(1, tm_cap // t))
        tm = gb * t
        blk = (gb, t, d)
        hmap = lambda i: (i, 0, 0)
        mmap = lambda i: (1, i, 0, 0)
        grid0, mblk = b // gb, (3, gb, 1, d)
    row = lambda a: pl.BlockSpec((tm, a.shape[1]), lambda i: (i, 0))
    full = lambda a: pl.BlockSpec(a.shape, lambda i: (0,) * a.ndim)
    ins = [o_mla] + ([wuv_bd] if decode else []) + [o_nsa, o_fox, w_out, h, mods, g_post.reshape(1, d)]
    specs = [row(o_mla)] + ([full(wuv_bd)] if decode else []) + \
            [row(o_nsa), row(o_fox), full(w_out), pl.BlockSpec(blk, hmap), pl.BlockSpec(mblk, mmap),
             pl.BlockSpec((1, d), lambda i: (0, 0))]
    return pl.pallas_call(
        functools.partial(_outproj_body, decode),
        grid=(grid0,),
        in_specs=specs,
        out_specs=pl.BlockSpec(blk, hmap),
        out_shape=jax.ShapeDtypeStruct(h.shape, F32),
        compiler_params=_cparams(("arbitrary",)),
        name="out_proj_dec" if decode else "out_proj",
    )(*ins)


NEW_PAD = LANES


def _page_copy(spec, layer, page, slot, j):
    cache_ref, buf, sem, src_rows, mode, width = spec
    src = cache_ref.at[layer, page]
    if src_rows is not None:
        src = src.at[pl.ds(src_rows[0], src_rows[1])]
    if mode == "lanes":
        dst = buf.at[slot, :, pl.ds(j * width, width)]
    elif mode == "mid":
        dst = buf.at[slot, :, j, :]
    else:
        dst = buf.at[slot, pl.ds(j * width, width)]
    return pltpu.make_async_copy(src, dst, sem.at[slot])


def _fetch_pages(pt_ref, bb, slot, layer, n_pages, specs):
    def body(j, c):
        page = pt_ref[bb, j]
        for k, spec in enumerate(specs):
            _page_copy(spec, layer, page, slot, j).start(priority=k % 2)
        return c

    lax.fori_loop(0, n_pages, body, 0, unroll=min(8, n_pages))


def _wait_pages(slot, layer, n_pages, specs):
    for j in range(n_pages):
        for spec in specs:
            _page_copy(spec, layer, 0, slot, j).wait()


def _gather_step(pt_ref, layer, n_pages, specs):
    b = pl.program_id(0)
    slot = b % 2

    @pl.when(b == 0)
    def _():
        _fetch_pages(pt_ref, 0, 0, layer, n_pages, specs)

    @pl.when(b + 1 < pl.num_programs(0))
    def _():
        _fetch_pages(pt_ref, b + 1, 1 - slot, layer, n_pages, specs)

    _wait_pages(slot, layer, n_pages, specs)
    return slot


def _pad_new(x):
    dt, w = x.shape
    return jnp.concatenate([x, jnp.zeros((NEW_PAD - dt, w), x.dtype)], axis=0).astype(BF16)


def _new_bias(rows, dt, ok=None):
    r = lax.broadcasted_iota(jnp.int32, (rows, NEW_PAD), 0)
    c = lax.broadcasted_iota(jnp.int32, (rows, NEW_PAD), 1)
    vis = c <= (r % dt)
    if ok is not None:
        vis = vis & ok
    return jnp.where(vis, 0.0, -jnp.inf)


def _joint_softmax_pv(sa, sb, va, vb, va_t=False):
    m = jnp.maximum(jnp.max(sa, axis=-1, keepdims=True), jnp.max(sb, axis=-1, keepdims=True))
    m = jnp.where(m > -jnp.inf, m, 0.0)
    ea, eb = jnp.exp(sa - m), jnp.exp(sb - m)
    d = jnp.sum(ea, axis=-1, keepdims=True) + jnp.sum(eb, axis=-1, keepdims=True)
    oa = _dot_nt(ea.astype(BF16), va) if va_t else _dot(ea.astype(BF16), va)
    o = oa + _dot(eb.astype(BF16), vb)
    return o * (1.0 / jnp.where(d > 0, d, 1.0))


def _mla_dec_body(layer, n_pages, page, pt_ref, qlat_ref, qx_ref, ckvn_ref, misc_ref, ckv_hbm, kr_hbm, o_ref,
                  ckv_buf, kr_buf, sem_c, sem_r):
    dt = qlat_ref.shape[1]
    slot = _gather_step(pt_ref, layer, n_pages, [(ckv_hbm, ckv_buf, sem_c, None, "rows", page),
                                                 (kr_hbm, kr_buf, sem_r, None, "lanes", page)])
    ql = qlat_ref[0]
    q = jnp.concatenate([ql[:, h * KV_LORA:(h + 1) * KV_LORA] for h in range(H_MLA)], axis=0).astype(BF16)
    lane = lax.broadcasted_iota(jnp.int32, (dt, LANES), 1)
    qxf = qx_ref[0]
    parts = []
    for h in range(H_MLA):
        x = qxf[:, (h // 2) * LANES:(h // 2 + 1) * LANES]
        if h % 2:
            x = pltpu.roll(x, LANES - ROPE_DIM, 1)
        parts.append(jnp.where(lane < ROPE_DIM, x, 0.0))
    qr = jnp.concatenate(parts, axis=0)[:, :ROPE_DIM].astype(BF16)
    ckv16 = ckv_buf[slot].astype(BF16)
    krt16 = kr_buf[slot].astype(BF16)
    ckvn16 = _pad_new(ckvn_ref[0])
    krn16 = _pad_new(misc_ref[0][:, :ROPE_DIM])
    sa = _dot_nt(q, ckv16) + _dot(qr, krt16)
    sb = _dot_nt(q, ckvn16) + _dot_nt(qr, krn16) + _new_bias(H_MLA * dt, dt)
    o = _joint_softmax_pv(sa, sb, ckv16, ckvn16)
    for h in range(H_MLA):
        o_ref[0, :, h * KV_LORA:(h + 1) * KV_LORA] = o[h * dt:(h + 1) * dt].astype(o_ref.dtype)


def mla_decode(layer, page_table, qlat, qx, ckv_new, misc, cache_ckv, cache_krt):
    db, dt, _ = qlat.shape
    n_pages, page = page_table.shape[1], cache_ckv.shape[2]
    past = n_pages * page
    blk = lambda w: pl.BlockSpec((1, dt, w), lambda b, pt: (b, 0, 0))
    any_spec = pl.BlockSpec(memory_space=pl.ANY)
    return pl.pallas_call(
        functools.partial(_mla_dec_body, layer, n_pages, page),
        grid_spec=pltpu.PrefetchScalarGridSpec(
            num_scalar_prefetch=1, grid=(db,),
            in_specs=[blk(H_MLA * KV_LORA), blk(qx.shape[-1]), blk(KV_LORA), blk(LANES), any_spec, any_spec],
            out_specs=blk(H_MLA * KV_LORA),
            scratch_shapes=[pltpu.VMEM((2, past, KV_LORA), F32), pltpu.VMEM((2, ROPE_DIM, past), F32),
                            pltpu.SemaphoreType.DMA((2,)), pltpu.SemaphoreType.DMA((2,))]),
        out_shape=jax.ShapeDtypeStruct((db, dt, H_MLA * KV_LORA), F32),
        compiler_params=_cparams(("arbitrary",)),
        name="mla_decode",
    )(page_table, qlat, qx, ckv_new, misc, cache_ckv, cache_krt)


def _fox_dec_body(layer, n_pages, page, pt_ref, fq_ref, fnew_ref, fcum_ref, kv_hbm, lf_hbm, o_ref,
                  kt_buf, vt_buf, lf_buf, sem_k, sem_v, sem_l):
    dt = fq_ref.shape[1]
    rows = H_FOX * dt
    hw = H_FOX * FOX_DIM
    slot = _gather_step(pt_ref, layer, n_pages, [(kv_hbm, kt_buf, sem_k, (0, hw), "lanes", page),
                                                 (kv_hbm, vt_buf, sem_v, (hw, hw), "lanes", page),
                                                 (lf_hbm, lf_buf, sem_l, None, "rows", H_FOX)])
    lane = lax.broadcasted_iota(jnp.int32, (dt, hw), 1)
    qf = fq_ref[0]
    q = jnp.concatenate([jnp.where(lane // FOX_DIM == h, qf, 0.0) for h in range(H_FOX)], axis=0).astype(BF16)
    kt16 = kt_buf[slot].astype(BF16)
    vt16 = vt_buf[slot].astype(BF16)
    fnew = fnew_ref[0]
    kn16, vn16 = _pad_new(fnew[:, :hw]), _pad_new(fnew[:, hw:])
    nr = n_pages * H_FOX
    lf = lf_buf[slot]
    ri = lax.broadcasted_iota(jnp.int32, (page, page), 0)
    ci = lax.broadcasted_iota(jnp.int32, (page, page), 1)
    tri = jnp.where(ri > ci, 1.0, 0.0).astype(BF16)
    ones = jnp.ones((page, page), BF16)
    pr = lax.broadcasted_iota(jnp.int32, (nr, nr), 0)
    pc = lax.broadcasted_iota(jnp.int32, (nr, nr), 1)
    upper = jnp.where((pc % H_FOX == pr % H_FOX) & (pc // H_FOX > pr // H_FOX), 1.0, 0.0).astype(BF16)
    l3 = _split3(lf)
    within = _dot(l3[0], tri) + _dot(l3[1], tri) + _dot(l3[2], tri)
    tot = _dot(l3[0], ones) + _dot(l3[1], ones) + _dot(l3[2], ones)
    t3 = _split3(tot)
    r2 = within + _dot(upper, t3[0]) + _dot(upper, t3[1]) + _dot(upper, t3[2])
    bias_rows = []
    for h in range(H_FOX):
        flat = jnp.concatenate([r2[j * H_FOX + h:j * H_FOX + h + 1, :] for j in range(n_pages)], axis=1)
        bias_rows.append(jnp.broadcast_to(flat, (dt, flat.shape[1])))
    fc = fcum_ref[0]
    gt_col = jnp.concatenate([fc[:, MISC_LOGF + h:MISC_LOGF + h + 1] for h in range(H_FOX)], axis=0)
    sa = _dot(q, kt16) + jnp.concatenate(bias_rows, axis=0) + gt_col
    lane_r = lax.broadcasted_iota(jnp.int32, (rows, LANES), 1)
    row_r = lax.broadcasted_iota(jnp.int32, (rows, LANES), 0)
    onehot = jnp.where(lane_r == MISC_LOGF + row_r // dt, 1.0, 0.0).astype(BF16)
    f3 = _split3(jnp.concatenate([fc, jnp.zeros((NEW_PAD - dt, LANES), F32)], axis=0))
    gt_row = _dot_nt(onehot, f3[0]) + _dot_nt(onehot, f3[1]) + _dot_nt(onehot, f3[2])
    sb = _dot_nt(q, kn16) + gt_col - gt_row + _new_bias(rows, dt)
    o = _joint_softmax_pv(sa, sb, vt16, vn16, va_t=True)
    out = jnp.zeros((dt, hw), F32)
    for h in range(H_FOX):
        out = out + jnp.where(lane // FOX_DIM == h, o[h * dt:(h + 1) * dt], 0.0)
    o_ref[0] = out.astype(o_ref.dtype)


def fox_decode(layer, page_table, fq, fox_new, fcum, cache_kvt, cache_lft):
    db, dt, hw = fq.shape
    n_pages, page = page_table.shape[1], cache_kvt.shape[3]
    past = n_pages * page
    blk = lambda w: pl.BlockSpec((1, dt, w), lambda b, pt: (b, 0, 0))
    any_spec = pl.BlockSpec(memory_space=pl.ANY)
    return pl.pallas_call(
        functools.partial(_fox_dec_body, layer, n_pages, page),
        grid_spec=pltpu.PrefetchScalarGridSpec(
            num_scalar_prefetch=1, grid=(db,),
            in_specs=[blk(hw), blk(2 * hw), blk(LANES), any_spec, any_spec],
            out_specs=blk(hw),
            scratch_shapes=[pltpu.VMEM((2, hw, past), F32), pltpu.VMEM((2, hw, past), F32),
                            pltpu.VMEM((2, n_pages * H_FOX, page), F32),
                            pltpu.SemaphoreType.DMA((2,)), pltpu.SemaphoreType.DMA((2,)),
                            pltpu.SemaphoreType.DMA((2,))]),
        out_shape=jax.ShapeDtypeStruct((db, dt, hw), F32),
        compiler_params=_cparams(("arbitrary",)),
        name="fox_decode",
    )(page_table, fq, fox_new, fcum, cache_kvt, cache_lft)


def _cmp_weight_pairs(w_cmp):
    z = jnp.zeros((L_CMP, NSA_DIM, NSA_DIM), w_cmp.dtype)
    top = jnp.concatenate([w_cmp[0], z], axis=2)
    bot = jnp.concatenate([z, w_cmp[1]], axis=2)
    return jnp.concatenate([top, bot], axis=1).reshape(L_CMP * LANES, LANES).astype(BF16)


def _cmp_weight_paged(w, bpp):
    eye = jnp.eye(bpp, dtype=w.dtype)
    v = jnp.einsum("lde,pq->dplqe", w, eye)
    return v.reshape(NSA_DIM * bpp * L_CMP, bpp * NSA_DIM).astype(BF16)


def _nsa_dec_body(layer, n_pages, page, n_sel, lane_bids, pt_ref, nq_ref, new_ref, wnew_ref, misc_ref, wpast_ref,
                  wl_ref, vk_ref, vv_ref, e_ref, bid_ref, nsa_hbm, o_ref, cmp_buf, sel_buf, xk_scr, xv_scr,
                  sem_c, sem_s):
    dt = nq_ref.shape[1]
    past = n_pages * page
    nbp = past // L_CMP
    nb = nbp + 1
    nbl = e_ref.shape[0]
    rows = H_NSA * dt
    d2 = 2 * NSA_DIM
    slot = _gather_step(pt_ref, layer, n_pages, [(nsa_hbm, cmp_buf, sem_c, (0, d2), "mid", None),
                                                 (nsa_hbm, sel_buf, sem_s, (d2, d2), "lanes", page)])
    lane = lax.broadcasted_iota(jnp.int32, (dt, LANES), 1)
    q = _head_slabs(nq_ref[0], lane)[:, :NSA_DIM].astype(BF16)
    new = new_ref[0]
    for d in range(NSA_DIM):
        xk_scr[:, d * page:(d + 1) * page] = cmp_buf[slot, d].astype(BF16)
        xv_scr[:, d * page:(d + 1) * page] = cmp_buf[slot, NSA_DIM + d].astype(BF16)
    sk = _dot(xk_scr[...], vk_ref[...])
    sv = _dot(xv_scr[...], vv_ref[...])
    part = jnp.zeros((1, LANES), F32)
    for l in range(dt):
        part = part + _dot(jnp.broadcast_to(new[l:l + 1, :LANES], (8, LANES)).astype(BF16),
                           wl_ref[l * LANES:(l + 1) * LANES, :])[0:1]
    tail_row = lax.broadcasted_iota(jnp.int32, (nbl - nbp, LANES), 0)
    tail = jnp.where(tail_row == 0, jnp.broadcast_to(part, (nbl - nbp, LANES)), 0.0)
    kc16 = jnp.concatenate([sk[:, :NSA_DIM], sk[:, NSA_DIM:], tail[:, :NSA_DIM]], axis=0).astype(BF16)
    vc16 = jnp.concatenate([sv[:, :NSA_DIM], sv[:, NSA_DIM:], tail[:, NSA_DIM:]], axis=0).astype(BF16)
    bid4 = jnp.broadcast_to(bid_ref[...], (rows, nbl))
    qpos4 = past + lax.broadcasted_iota(jnp.int32, (rows, nbl), 0) % dt
    sc = jnp.where((bid4 + 1) * L_CMP - 1 <= qpos4, jnp.where(bid4 < nb, _dot_nt(q, kc16), -jnp.inf), -jnp.inf)
    pc = _softmax_rows(sc)
    oc = _dot(pc.astype(BF16), vc16)
    imp = pc[0:dt]
    for h in range(1, H_NSA):
        imp = imp + pc[h * dt:(h + 1) * dt]
    bid = jnp.broadcast_to(bid_ref[...], (dt, nbl))
    cur = (past + lax.broadcasted_iota(jnp.int32, (dt, nbl), 0)) // L_CMP
    imp, validf = _block_importance(imp, bid, cur, nb)
    sel = _select_blocks(imp, validf, bid, lane_bids, n_sel)
    seltok = _dot(sel.astype(BF16), e_ref[...])
    bias = jnp.where(seltok > 0.5, 0.0, -jnp.inf)
    kvt16 = sel_buf[slot].astype(BF16)
    kn16 = _pad_new(new[:, 2 * NSA_DIM:3 * NSA_DIM])
    vn16 = _pad_new(new[:, 3 * NSA_DIM:])
    new_lane = 2 * (nbp // 2) if nbp % 2 == 0 else None
    sel_new = jnp.concatenate([sel[:, new_lane:new_lane + 1]] * H_NSA, axis=0) > 0.5
    sa = _dot(q, kvt16[:NSA_DIM]) + jnp.concatenate([bias] * H_NSA, axis=0)
    sb = _dot_nt(q, kn16) + _new_bias(rows, dt, sel_new)
    osel = _joint_softmax_pv(sa, sb, kvt16[NSA_DIM:], vn16, va_t=True)
    wpt16 = wpast_ref[0, 0].astype(BF16)
    wn = wnew_ref[0]
    wb = wpt16.shape[1]
    qpw = past + lax.broadcasted_iota(jnp.int32, (rows, wb), 0) % dt
    kpw = past - wb + lax.broadcasted_iota(jnp.int32, (rows, wb), 1)
    okw = (qpw - kpw <= WINDOW) & (kpw >= 0)
    sa = jnp.where(okw, _dot(q, wpt16[:NSA_DIM]), -jnp.inf)
    sb = _dot_nt(q, _pad_new(wn[:, :NSA_DIM])) + _new_bias(rows, dt)
    ow = _joint_softmax_pv(sa, sb, wpt16[NSA_DIM:], _pad_new(wn[:, NSA_DIM:]), va_t=True)
    g = misc_ref[0]
    heads = []
    for h in range(H_NSA):
        sl = slice(h * dt, (h + 1) * dt)
        c0 = MISC_GATE + 3 * h
        heads.append(g[:, c0:c0 + 1] * oc[sl] + g[:, c0 + 1:c0 + 2] * osel[sl] + g[:, c0 + 2:c0 + 3] * ow[sl])
    o_ref[0] = jnp.concatenate(heads, axis=1).astype(o_ref.dtype)


def nsa_decode(layer, page_table, nq, nsa_new, win_new, misc, win_t, wl, vk, vv, cache_nsat):
    db, dt, _ = nq.shape
    n_pages, page = page_table.shape[1], cache_nsat.shape[3]
    past = n_pages * page
    assert page == 2 * L_CMP and dt <= min(L_CMP, WINDOW, NEW_PAD)
    wb = win_t.shape[3]
    nbp = past // L_CMP
    nb = nbp + 1
    nbl = -(-nb // LANES) * LANES
    n_sel = min(N_SEL, nb)
    bids = np.full((nbl,), nbl + nb, np.int32)
    bids[:n_pages] = 2 * np.arange(n_pages)
    bids[n_pages:2 * n_pages] = 2 * np.arange(n_pages) + 1
    bids[2 * n_pages] = nbp
    lane_bids = tuple((int(i), int(bids[i])) for i in range(2 * n_pages + 1))
    e = (jnp.arange(past)[None, :] // L_CMP == jnp.asarray(bids)[:, None]).astype(BF16)
    blk = lambda w: pl.BlockSpec((1, dt, w), lambda b, pt: (b, 0, 0))
    full2 = lambda a: pl.BlockSpec(a.shape, lambda b, pt: (0, 0))
    bid_arr = jnp.asarray(bids).reshape(1, nbl)
    return pl.pallas_call(
        functools.partial(_nsa_dec_body, layer, n_pages, page, n_sel, lane_bids),
        grid_spec=pltpu.PrefetchScalarGridSpec(
            num_scalar_prefetch=1, grid=(db,),
            in_specs=[blk(256), blk(256), blk(LANES), blk(LANES),
                      pl.BlockSpec((1, 1, 2 * NSA_DIM, wb), lambda b, pt: (layer, b, 0, 0)),
                      full2(wl), full2(vk), full2(vv), full2(e), full2(bid_arr),
                      pl.BlockSpec(memory_space=pl.ANY)],
            out_specs=blk(256),
            scratch_shapes=[pltpu.VMEM((2, 2 * NSA_DIM, n_pages, page), F32), pltpu.VMEM((2, 2 * NSA_DIM, past), F32),
                            pltpu.VMEM((n_pages, NSA_DIM * page), BF16), pltpu.VMEM((n_pages, NSA_DIM * page), BF16),
                            pltpu.SemaphoreType.DMA((2,)), pltpu.SemaphoreType.DMA((2,))]),
        out_shape=jax.ShapeDtypeStruct((db, dt, 256), F32),
        compiler_params=_cparams(("arbitrary",)),
        name="nsa_decode",
    )(page_table, nq, nsa_new, win_new, misc, win_t, wl, vk, vv, e, bid_arr, cache_nsat)


def _mixer_prompt(h, mods, g_pre, g_post, lw, consts):
    b, t, d = h.shape
    z = in_proj(h, mods, g_pre, lw, consts["tab_p"], lw["bvec"], consts["ltri_p"], consts["place"], decode=False)
    r3 = lambda a: a.reshape(b, t, a.shape[-1])
    o_mla = flash_pairs(r3(z["qn"]), r3(z["qx"]), r3(z["kn"]), r3(z["kx"]), r3(z["vm"]),
                        xw=ROPE_DIM, kx_shared=True)
    fkv16 = r3(z["fkv16"])
    o_fox = flash_pairs(r3(z["fq"]), r3(z["fqx"]), fkv16, r3(z["fkx"]), fkv16,
                        xw=FOX_XW, kx_shared=False, v_off=H_FOX // 2)
    nb = t // L_CMP
    cmpkv = compress_blocks(z["nsa"].reshape(b * nb, L_CMP * 256), lw["wbig"]).reshape(b, nb, LANES)
    cmpkv = jnp.pad(cmpkv, ((0, 0), (0, LANES - nb), (0, 0)))
    o_nsa = nsa_prompt(r3(z["nq"]), cmpkv, r3(z["nsa"]), r3(z["win"]), r3(z["misc"]))
    n = b * t
    h_new = out_proj(o_mla.reshape(n, -1), o_nsa.reshape(n, -1), o_fox.reshape(n, -1), lw["w_out"], h, mods, g_post)
    wb = min(WINDOW, t)
    misc = r3(z["misc"])
    state = (r3(z["ckv"]), misc[:, :, :ROPE_DIM], r3(z["nsa"]).reshape(b, t, 4, NSA_DIM),
             r3(z["win"])[:, t - wb:].reshape(b, wb, 2, NSA_DIM),
             r3(z["fox"]).reshape(b, t, 2, H_FOX, FOX_DIM), misc[:, :, MISC_LOGF:MISC_END])
    return h_new, state


def _mixer_sample(h, mods, g_pre, g_post, lw, consts, l, page_table, c_ckv, c_kr, c_nsa, s_win, c_fkv, c_flf):
    db, dt, d = h.shape
    z = in_proj(h, mods, g_pre, lw, consts["tab_s"], lw["bvec"], consts["ltri_s"], consts["place"], decode=True)
    r3 = lambda a: a.reshape(db, dt, a.shape[-1])
    misc = r3(z["misc"])
    o_lat = mla_decode(l, page_table, r3(z["qlat"]), r3(z["qx"]), r3(z["ckv"]), misc, c_ckv, c_kr)
    o_nsa = nsa_decode(l, page_table, r3(z["nq"]), r3(z["nsa"]), r3(z["win"]), misc, consts["win_t"], lw["wl"],
                       lw["vk"], lw["vv"], c_nsa)
    o_fox = fox_decode(l, page_table, r3(z["fq"]), r3(z["fox"]), r3(z["fcum"]), c_fkv, c_flf)
    win_new = jnp.concatenate([s_win[l, :, dt:], r3(z["win"]).reshape(db, dt, 2, NSA_DIM)], axis=1)
    n = db * dt
    h_new = out_proj(o_lat.reshape(n, -1), o_nsa.reshape(n, -1), o_fox.reshape(n, -1), lw["w_out"], h, mods, g_post,
                     wuv_bd=lw["wuv_bd"])
    state = (r3(z["ckv"]), misc[:, :, :ROPE_DIM], r3(z["nsa"]).reshape(db, dt, 4, NSA_DIM),
             win_new, r3(z["fox"]).reshape(db, dt, 2, H_FOX, FOX_DIM),
             misc[:, :, MISC_LOGF:MISC_END])
    return h_new, state


def _layer_weights(l, w_in, b_fox_f, mla_g_q, mla_g_kv, mla_w_uq, mla_w_uk, mla_w_uv, nsa_w_cmp, w_out):
    lw = _inproj_weights(w_in[l], mla_w_uq[l], mla_w_uk[l], mla_w_uv[l])
    bvec = jnp.zeros((1, LANES), F32).at[0, MISC_LOGF:MISC_END].set(b_fox_f[l])
    eye = jnp.eye(H_MLA, dtype=F32)
    wuv_bd = jnp.einsum("chd,hg->hcgd", mla_w_uv[l], eye).reshape(H_MLA * KV_LORA, H_MLA * V_DIM)
    lw.update(g_q=mla_g_q[l], g_kv=mla_g_kv[l], bvec=bvec, wbig=_cmp_weight(nsa_w_cmp[l]),
              w_out=w_out[l].astype(BF16), wuv_bd=wuv_bd.astype(BF16), wl=_cmp_weight_pairs(nsa_w_cmp[l]),
              vk=_cmp_weight_paged(nsa_w_cmp[l, 0], 2), vv=_cmp_weight_paged(nsa_w_cmp[l, 1], 2))
    return lw


def kernel(x_prompt, x_sample, cache_mla_ckv, cache_mla_krope, cache_nsa_kv, state_nsa_win, cache_fox_kv,
           cache_fox_logf, page_table, c_prompt, c_sample, ada_w, ada_b, norm_pre, norm_post, ffn_w_gate,
           ffn_w_up, ffn_w_down, w_in, b_fox_f, mla_g_q, mla_g_kv, mla_w_uq, mla_w_uk, mla_w_uv, nsa_w_cmp, w_out):
    depth = w_in.shape[0]
    b, t, d = x_prompt.shape
    db, dt, _ = x_sample.shape
    past = page_table.shape[1] * cache_mla_ckv.shape[2]
    mods_all = ada_mods(jnp.concatenate([c_prompt, c_sample], axis=0), ada_w, ada_b)
    wg16, wu16, wd16 = ffn_w_gate.astype(BF16), ffn_w_up.astype(BF16), ffn_w_down.astype(BF16)
    tm_p = min(t, 512)
    gb = min(db, max(1, 512 // dt))
    tm_s = gb * dt
    ii = jnp.arange(tm_s)
    consts = dict(
        tab_p=_rope_tables(jnp.arange(t, dtype=jnp.int32)),
        tab_s=jnp.tile(_rope_tables(past + jnp.arange(dt, dtype=jnp.int32)), (gb, 1)),
        ltri_p=(jnp.arange(tm_p)[:, None] >= jnp.arange(tm_p)[None, :]).astype(BF16),
        ltri_s=((ii[:, None] >= ii[None, :]) & (ii[:, None] // dt == ii[None, :] // dt)).astype(BF16),
        place=_fox_place(),
        win_t=jnp.transpose(state_nsa_win, (0, 1, 3, 4, 2)).reshape(depth, db, 2 * NSA_DIM, -1))
    pool, page = cache_mla_ckv.shape[1:3]
    cache_krt = jnp.transpose(cache_mla_krope, (0, 1, 3, 2))
    cache_nsat = jnp.transpose(cache_nsa_kv, (0, 1, 3, 4, 2)).reshape(depth, pool, 4 * NSA_DIM, page)
    cache_fkvt = jnp.transpose(cache_fox_kv, (0, 1, 3, 4, 5, 2)).reshape(depth, pool, 2 * H_FOX * FOX_DIM, page)
    cache_lft = jnp.transpose(cache_fox_logf, (0, 1, 3, 2))
    hp, hs = x_prompt, x_sample
    st_p, st_s = [], []
    for l in range(depth):
        lw = _layer_weights(l, w_in, b_fox_f, mla_g_q, mla_g_kv, mla_w_uq, mla_w_uk, mla_w_uv, nsa_w_cmp, w_out)
        mp = mods_all[l, :, :b].reshape(3 * N_SUB, b, 1, d)
        ms = mods_all[l, :, b:].reshape(3 * N_SUB, db, 1, d)
        hp = ffn_half(hp, mp, 0, norm_pre[l, 0], norm_post[l, 0], wg16, wu16, wd16, l, 0)
        hs = ffn_half(hs, ms, 0, norm_pre[l, 0], norm_post[l, 0], wg16, wu16, wd16, l, 0)
        hp, sp = _mixer_prompt(hp, mp, norm_pre[l, 1], norm_post[l, 1], lw, consts)
        hs, ss = _mixer_sample(hs, ms, norm_pre[l, 1], norm_post[l, 1], lw, consts, l, page_table,
                               cache_mla_ckv, cache_krt, cache_nsat, state_nsa_win, cache_fkvt, cache_lft)
        hp = ffn_half(hp, mp, 2, norm_pre[l, 2], norm_post[l, 2], wg16, wu16, wd16, l, 1)
        hs = ffn_half(hs, ms, 2, norm_pre[l, 2], norm_post[l, 2], wg16, wu16, wd16, l, 1)
        st_p.append(sp)
        st_s.append(ss)
    outs_p = tuple(jnp.stack(a) for a in zip(*st_p))
    outs_s = tuple(jnp.stack(a) for a in zip(*st_s))
    return (hp, hs) + outs_p + outs_s
```
